```python
import numpy as np
import jax
import jax.numpy as jnp
from jax import lax

D_MODEL = 1024
BATCH = 8
SEQ = 2048
DEPTH = 4
DEC_BATCH = 128
DEC_SEQ = 1
PAST_LEN = 16384
PAGE_SIZE = 128

N_META = 16
CHUNK = 128
RET_HEADS = 4
RET_DK = 256
RET_DV = 512
RET_QK = RET_HEADS * RET_DK
RET_V = RET_HEADS * RET_DV
ROPE_BASE = 10000.0
RW_HEADS = 16
RW_N = 64
RW_C = RW_HEADS * RW_N
LORA_W = 64
LORA_A = 64
LORA_V = 32
LORA_G = 128
RWKV_IN = 3 * RW_C + LORA_W + LORA_A + LORA_G
C_IN = 2 * RET_QK + 2 * RET_V + RWKV_IN + 2 * D_MODEL
D_FF = ((8 * D_MODEL + 3 * 256 - 1) // (3 * 256)) * 256
NORM_EPS = 1e-6
LNX_EPS = 64e-5

kernel_name = 'retnet_rwkv7_parallel_hybrid_step'


def _split(x, sizes):
    return jnp.split(x, np.cumsum(np.array(sizes))[:-1].tolist(), axis=-1)


def _rmsnorm(x, gain):
    xf = x.astype(jnp.float32)
    y = xf * lax.rsqrt(jnp.mean(xf * xf, axis=-1, keepdims=True) + NORM_EPS)
    return (y * gain.astype(jnp.float32)).astype(x.dtype)


def _rope(x, pos):
    half = x.shape[-1] // 2
    inv_freq = 1.0 / (ROPE_BASE ** (jnp.arange(half, dtype=jnp.float32) / half))
    ang = pos.astype(jnp.float32)[:, None] * inv_freq[None, :]
    cos = jnp.cos(ang)[None, :, None, :]
    sin = jnp.sin(ang)[None, :, None, :]
    x1 = x[..., :half]
    x2 = x[..., half:]
    return jnp.concatenate([x1 * cos - x2 * sin, x2 * cos + x1 * sin], axis=-1)


def _log_gamma():
    return jnp.log(1.0 - 2.0 ** (-5.0 - jnp.arange(RET_HEADS, dtype=jnp.float32)))


def _retention_chunk(q, k, v, s, log_gamma):
    L = q.shape[1]
    idx = jnp.arange(L, dtype=jnp.float32)
    rel = idx[:, None] - idx[None, :]
    dmask = jnp.where(rel[None] >= 0, jnp.exp(log_gamma[:, None, None] * jnp.maximum(rel, 0.0)[None]), 0.0)
    scores = jnp.einsum('blhd,bmhd->bhlm', q, k) * dmask[None]
    inner = jnp.einsum('bhlm,bmhe->blhe', scores, v)
    q_decay = jnp.exp((idx[:, None] + 1.0) * log_gamma[None, :])
    cross = jnp.einsum('blhd,bhde->blhe', q, s) * q_decay[None, :, :, None]
    k_decay = jnp.exp((L - 1.0 - idx[:, None]) * log_gamma[None, :])
    s_new = (jnp.exp(L * log_gamma)[None, :, None, None] * s
             + jnp.einsum('blhd,blhe->bhde', k * k_decay[None, :, :, None], v))
    return inner + cross, s_new


def _retention_prompt(q, k, v, s0, log_gamma):
    b = q.shape[0]
    o_meta, s = _retention_chunk(q[:, :N_META], k[:, :N_META], v[:, :N_META], s0, log_gamma)

    def chunks(t):
        t = t[:, N_META:]
        return t.reshape(b, t.shape[1] // CHUNK, CHUNK, t.shape[2], t.shape[3]).swapaxes(0, 1)

    def step(s_c, qkv):
        o_c, s_n = _retention_chunk(qkv[0], qkv[1], qkv[2], s_c, log_gamma)
        return s_n, o_c

    s, o_c = lax.scan(step, s, (chunks(q), chunks(k), chunks(v)))
    o_c = o_c.swapaxes(0, 1)
    o_c = o_c.reshape(b, -1, o_c.shape[3], o_c.shape[4])
    return jnp.concatenate([o_meta, o_c], axis=1), s


def _wkv7_scan(s0, r, w, k, v, kk, a):
    xs = tuple(u.swapaxes(0, 1) for u in (r, w, k, v, kk, a))

    def step(s, inp):
        r_t, w_t, k_t, v_t, kk_t, a_t = inp
        s_kk = jnp.einsum('bhvk,bhk->bhv', s, kk_t)
        s = (s * w_t[:, :, None, :] - s_kk[..., None] * (kk_t * a_t)[:, :, None, :]
             + v_t[..., None] * k_t[:, :, None, :])
        y = jnp.einsum('bhvk,bhk->bhv', s, r_t)
        return s, y

    s, ys = lax.scan(step, s0, xs)
    return ys.swapaxes(0, 1), s


def _rwkv7_branch(rw, prev_row, s0, v_first, l, p):
    b, t, _ = rw.shape
    f32 = jnp.float32
    prev = jnp.concatenate([prev_row[:, None, :].astype(rw.dtype), rw[:, :-1]], axis=1)
    z = rw + (prev - rw) * p['rwkv_mu'][l]
    r, k, v, wd, ad, gd = _split(z, (RW_C, RW_C, RW_C, LORA_W, LORA_A, LORA_G))
    w_log = -jax.nn.softplus(-(p['rwkv_w0'][l] + jnp.tanh(wd) @ p['rwkv_w2'][l])) - 0.5
    decay = jnp.exp(-jnp.exp(w_log.astype(f32)))
    a = jax.nn.sigmoid(p['rwkv_a0'][l] + ad @ p['rwkv_a2'][l])
    g = jax.nn.sigmoid(gd) @ p['rwkv_g2'][l]
    if l == 0:
        v_first = v
    else:
        v = v + (v_first - v) * jax.nn.sigmoid(
            p['rwkv_v0'][l - 1] + (v @ p['rwkv_v1'][l - 1]) @ p['rwkv_v2'][l - 1])

    def heads(u):
        return u.astype(f32).reshape(b, t, RW_HEADS, RW_N)

    kk = heads(k * p['rwkv_kk'][l])
    kk = kk / jnp.maximum(jnp.sqrt(jnp.sum(kk * kk, axis=-1, keepdims=True)), 1e-12)
    k = k * (1.0 + (a - 1.0) * p['rwkv_ka'][l])
    rh = heads(r)
    kh = heads(k)
    vh = heads(v)
    ah = heads(a)
    wh = decay.reshape(b, t, RW_HEADS, RW_N)
    y, s_new = _wkv7_scan(s0.astype(f32), rh, wh, kh, vh, kk, ah)
    mean = jnp.mean(y, axis=-1, keepdims=True)
    var = jnp.mean(jnp.square(y - mean), axis=-1, keepdims=True)
    yn = ((y - mean) * lax.rsqrt(var + LNX_EPS)).reshape(b, t, RW_C)
    yn = yn * p['rwkv_lnx_w'][l].astype(f32) + p['rwkv_lnx_b'][l].astype(f32)
    r_k = p['rwkv_rk'][l].astype(f32).reshape(RW_HEADS, RW_N)
    bonus = (jnp.sum(rh * kh * r_k, axis=-1, keepdims=True) * vh).reshape(b, t, RW_C)
    out = ((yn + bonus) * g.astype(f32)).astype(rw.dtype) @ p['w_rwkv_out'][l]
    return out, s_new, rw[:, -1], v_first


def _block(x, pos, s_ret, s_wkv, prev_row, v_first, l, is_prompt, p):
    b, t, _ = x.shape
    gains = p['norm_gain'][l]
    xn = _rmsnorm(x, gains[0])
    proj = xn @ p['w_in'][l]
    qr, kr, vr, gr, rw, gates = _split(proj, (RET_QK, RET_QK, RET_V, RET_V, RWKV_IN, 2 * D_MODEL))
    log_gamma = _log_gamma()
    q = _rope(qr.astype(jnp.float32).reshape(b, t, RET_HEADS, RET_DK), pos)
    k = _rope(kr.astype(jnp.float32).reshape(b, t, RET_HEADS, RET_DK), pos) * (RET_DK ** -0.5)
    v = vr.astype(jnp.float32).reshape(b, t, RET_HEADS, RET_DV)
    if is_prompt:
        o, s_ret_new = _retention_prompt(q, k, v, s_ret.astype(jnp.float32), log_gamma)
    else:
        o, s_ret_new = _retention_chunk(q, k, v, s_ret.astype(jnp.float32), log_gamma)
    o = o * lax.rsqrt(jnp.mean(o * o, axis=-1, keepdims=True) + NORM_EPS)
    o = o.reshape(b, t, RET_V).astype(x.dtype) * jax.nn.silu(gr)
    y_ret = o @ p['w_ret_out'][l]
    y_rwkv, s_wkv_new, last_row, v_first = _rwkv7_branch(rw, prev_row, s_wkv, v_first, l, p)
    g_ret, g_rwkv = jnp.split(jax.nn.sigmoid(gates), 2, axis=-1)
    mix = (g_ret * y_ret + g_rwkv * y_rwkv) @ p['w_out'][l]
    x = x + _rmsnorm(mix, gains[1])
    hn = _rmsnorm(x, gains[2])
    f = (jax.nn.silu(hn @ p['ffn_w1'][l]) * (hn @ p['ffn_w3'][l])) @ p['ffn_w2'][l]
    x = x + _rmsnorm(f, gains[3])
    return x, s_ret_new, s_wkv_new, last_row, v_first


def setup_inputs(seed: int = 0) -> dict:
    key = jax.random.key(seed)
    ks = jax.random.split(key, 32)
    f32 = jnp.float32

    def nrm(k, shape, scale):
        return jax.random.normal(k, shape, f32) * scale

    return {
        'x_prompt': nrm(ks[0], (BATCH, SEQ, D_MODEL), 1.0),
        'x_sample': nrm(ks[1], (DEC_BATCH, DEC_SEQ, D_MODEL), 1.0),
        'state_ret': nrm(ks[2], (DEPTH, DEC_BATCH, RET_HEADS, RET_DK, RET_DV), 0.3),
        'state_wkv': nrm(ks[3], (DEPTH, DEC_BATCH, RW_HEADS, RW_N, RW_N), 0.5),
        'state_shift': nrm(ks[4], (DEPTH, DEC_BATCH, RWKV_IN), 1.0),
        'meta_tokens': nrm(ks[5], (N_META, D_MODEL), 1.0),
        'norm_gain': 1.0 + nrm(ks[6], (DEPTH, 4, D_MODEL), 0.05),
        'w_in': nrm(ks[7], (DEPTH, D_MODEL, C_IN), D_MODEL ** -0.5),
        'w_ret_out': nrm(ks[8], (DEPTH, RET_V, D_MODEL), RET_V ** -0.5),
        'w_rwkv_out': nrm(ks[9], (DEPTH, RW_C, D_MODEL), RW_C ** -0.5),
        'w_out': nrm(ks[10], (DEPTH, D_MODEL, D_MODEL), D_MODEL ** -0.5),
        'rwkv_mu': jax.random.uniform(ks[11], (DEPTH, RWKV_IN), f32),
        'rwkv_w0': -2.0 + nrm(ks[12], (DEPTH, RW_C), 1.0),
        'rwkv_w2': nrm(ks[13], (DEPTH, LORA_W, RW_C), 0.5 * LORA_W ** -0.5),
        'rwkv_a0': nrm(ks[14], (DEPTH, RW_C), 0.1),
        'rwkv_a2': nrm(ks[15], (DEPTH, LORA_A, RW_C), 0.5 * LORA_A ** -0.5),
        'rwkv_g2': nrm(ks[16], (DEPTH, LORA_G, RW_C), LORA_G ** -0.5),
        'rwkv_kk': 0.85 + nrm(ks[17], (DEPTH, RW_C), 0.05),
        'rwkv_ka': 1.0 + nrm(ks[18], (DEPTH, RW_C), 0.05),
        'rwkv_rk': nrm(ks[19], (DEPTH, RW_C), 0.1),
        'rwkv_lnx_w': 1.0 + nrm(ks[20], (DEPTH, RW_C), 0.05),
        'rwkv_lnx_b': nrm(ks[21], (DEPTH, RW_C), 0.02),
        'rwkv_v0': 1.0 + nrm(ks[22], (DEPTH - 1, RW_C), 0.1),
        'rwkv_v1': nrm(ks[23], (DEPTH - 1, RW_C, LORA_V), RW_C ** -0.5),
        'rwkv_v2': nrm(ks[24], (DEPTH - 1, LORA_V, RW_C), 0.5 * LORA_V ** -0.5),
        'ffn_w1': nrm(ks[25], (DEPTH, D_MODEL, D_FF), D_MODEL ** -0.5),
        'ffn_w3': nrm(ks[26], (DEPTH, D_MODEL, D_FF), D_MODEL ** -0.5),
        'ffn_w2': nrm(ks[27], (DEPTH, D_FF, D_MODEL), D_FF ** -0.5),
    }


def reference(x_prompt, x_sample, state_ret, state_wkv, state_shift, meta_tokens, norm_gain,
              w_in, w_ret_out, w_rwkv_out, w_out, rwkv_mu, rwkv_w0, rwkv_w2, rwkv_a0, rwkv_a2,
              rwkv_g2, rwkv_kk, rwkv_ka, rwkv_rk, rwkv_lnx_w, rwkv_lnx_b, rwkv_v0, rwkv_v1, rwkv_v2,
              ffn_w1, ffn_w3, ffn_w2):
    p = dict(norm_gain=norm_gain, w_in=w_in, w_ret_out=w_ret_out, w_rwkv_out=w_rwkv_out,
             w_out=w_out, rwkv_mu=rwkv_mu, rwkv_w0=rwkv_w0, rwkv_w2=rwkv_w2, rwkv_a0=rwkv_a0,
             rwkv_a2=rwkv_a2, rwkv_g2=rwkv_g2, rwkv_kk=rwkv_kk, rwkv_ka=rwkv_ka, rwkv_rk=rwkv_rk,
             rwkv_lnx_w=rwkv_lnx_w, rwkv_lnx_b=rwkv_lnx_b, rwkv_v0=rwkv_v0, rwkv_v1=rwkv_v1,
             rwkv_v2=rwkv_v2, ffn_w1=ffn_w1, ffn_w3=ffn_w3, ffn_w2=ffn_w2)
    bp, sp, d = x_prompt.shape
    xp = jnp.concatenate([jnp.broadcast_to(meta_tokens.astype(x_prompt.dtype)[None], (bp, N_META, d)),
                          x_prompt], axis=1)
    xs = x_sample
    pos_p = jnp.arange(N_META + sp)
    pos_s = PAST_LEN + jnp.arange(x_sample.shape[1])
    zero_ret = jnp.zeros((bp, RET_HEADS, RET_DK, RET_DV), jnp.float32)
    zero_wkv = jnp.zeros((bp, RW_HEADS, RW_N, RW_N), jnp.float32)
    zero_shift = jnp.zeros((bp, RWKV_IN), x_prompt.dtype)
    vf_p = None
    vf_s = None
    ret_p, wkv_p, sh_p, ret_s, wkv_s, sh_s = [], [], [], [], [], []
    for l in range(DEPTH):
        xp, sr, sw, sh, vf_p = _block(xp, pos_p, zero_ret, zero_wkv, zero_shift, vf_p, l, True, p)
        ret_p.append(sr)
        wkv_p.append(sw)
        sh_p.append(sh)
        xs, sr, sw, sh, vf_s = _block(xs, pos_s, state_ret[l], state_wkv[l], state_shift[l], vf_s, l, False, p)
        ret_s.append(sr)
        wkv_s.append(sw)
        sh_s.append(sh)
    return (xp[:, N_META:], xs, jnp.stack(ret_p), jnp.stack(wkv_p), jnp.stack(sh_p),
            jnp.stack(ret_s), jnp.stack(wkv_s), jnp.stack(sh_s))
```

```python
import functools
import math

import jax
import jax.numpy as jnp
from jax import lax
from jax.experimental import pallas as pl
from jax.experimental.pallas import tpu as pltpu

F32 = jnp.float32
BF16 = jnp.bfloat16

NORM_EPS = 1e-6
LNX_EPS = 64e-5
ROPE_BASE = 10000.0
PAST_LEN = 16384
KK_EPS = 1e-12

CHUNK = 64
MXU_TILE = 256
VMEM_LIMIT = 56 * 1024 * 1024
PREP_TILES = (512, 256, 128, 64)


def _cparams(*sem):
    return pltpu.CompilerParams(dimension_semantics=sem, vmem_limit_bytes=VMEM_LIMIT)


def _dot(a, b):
    return jnp.dot(a, b, preferred_element_type=F32)


def _dot_nt(a, b):
    return lax.dot_general(a, b, (((1,), (1,)), ((), ())), preferred_element_type=F32)


def _dot_tn(a, b):
    return lax.dot_general(a, b, (((0,), (0,)), ((), ())), preferred_element_type=F32)


def _bf(x):
    return x.astype(BF16)


def _sigmoid(x):
    return 1.0 / (1.0 + jnp.exp(-x))


def _rms(x, gain):
    return x * lax.rsqrt(jnp.mean(x * x, axis=-1, keepdims=True) + NORM_EPS) * gain


def _group_ones(group):
    shift = group.bit_length() - 1
    r = lax.broadcasted_iota(jnp.int32, (MXU_TILE, MXU_TILE), 0) >> shift
    c = lax.broadcasted_iota(jnp.int32, (MXU_TILE, MXU_TILE), 1) >> shift
    return jnp.where(r == c, 1.0, 0.0).astype(BF16)


def _group_sum(x, ones):
    hi = _bf(x)
    lo = _bf(x - hi.astype(F32))
    parts = []
    for j in range(x.shape[1] // MXU_TILE):
        sl = slice(j * MXU_TILE, (j + 1) * MXU_TILE)
        parts.append(_dot(hi[:, sl], ones) + _dot(lo[:, sl], ones))
    return parts[0] if len(parts) == 1 else jnp.concatenate(parts, axis=1)


def _norm_proj_kernel(x_ref, g_ref, w_ref, o_ref, xn_ref):
    @pl.when(pl.program_id(1) == 0)
    def _():
        xn_ref[...] = _bf(_rms(x_ref[...], g_ref[...]))

    o_ref[...] = _dot(xn_ref[...], w_ref[...]).astype(o_ref.dtype)


def _norm_proj(x, gain, w, tm, tn):
    rows, d = x.shape
    n = w.shape[1]
    return pl.pallas_call(
        _norm_proj_kernel,
        out_shape=jax.ShapeDtypeStruct((rows, n), BF16),
        grid=(rows // tm, n // tn),
        in_specs=[pl.BlockSpec((tm, d), lambda i, j: (i, 0)),
                  pl.BlockSpec((1, d), lambda i, j: (0, 0)),
                  pl.BlockSpec((d, tn), lambda i, j: (0, j))],
        out_specs=pl.BlockSpec((tm, tn), lambda i, j: (i, j)),
        scratch_shapes=[pltpu.VMEM((tm, d), BF16)],
        compiler_params=_cparams("parallel", "arbitrary"),
        name="norm_proj",
    )(x, gain, w)


def _ret_chunk_kernel(q_ref, k_ref, v_ref, gr_ref, cos_ref, sin_ref, s0_ref, og_ref, sout_ref, s_scr,
                      *, heads, dk, dv, n_pad):
    c = pl.program_id(1)
    n_chunks = pl.num_programs(1)
    half = dk // 2

    @pl.when(c == 0)
    def _():
        s_scr[...] = s0_ref[...]

    cos = cos_ref[...]
    sin = sin_ref[...]
    row = lax.broadcasted_iota(jnp.int32, (CHUNK, half), 0)
    rowf = row.astype(F32)
    rel_i = lax.broadcasted_iota(jnp.int32, (CHUNK, CHUNK), 0)
    rel_j = lax.broadcasted_iota(jnp.int32, (CHUNK, CHUNK), 1)
    rel = (rel_i - rel_j).astype(F32)

    def rope(x):
        x1 = x[:, :half]
        x2 = x[:, half:]
        return x1 * cos - x2 * sin, x2 * cos + x1 * sin

    for h in range(heads):
        lg = math.log(1.0 - 2.0 ** (-5.0 - h))
        qs = slice(h * dk, (h + 1) * dk)
        vs = slice(h * dv, (h + 1) * dv)
        q1, q2 = rope(q_ref[:, qs].astype(F32))
        k1, k2 = rope(k_ref[:, qs].astype(F32))
        k1 = k1 * (dk ** -0.5)
        k2 = k2 * (dk ** -0.5)
        vh = v_ref[:, vs]
        if n_pad:
            k1 = jnp.where(row >= n_pad, k1, 0.0)
            k2 = jnp.where(row >= n_pad, k2, 0.0)
            vrow = lax.broadcasted_iota(jnp.int32, (CHUNK, dv), 0)
            vh = jnp.where(vrow >= n_pad, vh, jnp.zeros_like(vh))
        q_decay = jnp.exp((rowf + 1.0) * lg)
        k_decay = jnp.exp((CHUNK - 1.0 - rowf) * lg)
        dmask = jnp.where(rel >= 0, jnp.exp(lg * jnp.maximum(rel, 0.0)), 0.0)
        qb = _bf(jnp.concatenate([q1, q2], axis=1))
        kb = _bf(jnp.concatenate([k1, k2], axis=1))
        qd = _bf(jnp.concatenate([q1 * q_decay, q2 * q_decay], axis=1))
        kd = _bf(jnp.concatenate([k1 * k_decay, k2 * k_decay], axis=1))
        s_h = s_scr[h]
        scores = _dot_nt(qb, kb) * dmask
        o = _dot(_bf(scores), vh) + _dot(qd, _bf(s_h))
        o = o * lax.rsqrt(jnp.mean(o * o, axis=-1, keepdims=True) + NORM_EPS)
        g = gr_ref[:, vs].astype(F32)
        og_ref[:, vs] = _bf(o * (g * _sigmoid(g)))
        s_scr[h] = math.exp(CHUNK * lg) * s_h + _dot_tn(kd, vh)

    @pl.when(c == n_chunks - 1)
    def _():
        sout_ref[0] = s_scr[...]


def _ret_chunks(proj, cos, sin, s0, *, row_block0, n_seq, n_chunks, heads, dk, dv, n_pad):
    qk = heads * dk
    vw = heads * dv
    assert vw == 2 * qk

    def rowmap(col):
        return lambda b, c: (row_block0 + b * n_chunks + c, col)

    kern = functools.partial(_ret_chunk_kernel, heads=heads, dk=dk, dv=dv, n_pad=n_pad)
    return pl.pallas_call(
        kern,
        out_shape=(jax.ShapeDtypeStruct((n_seq * n_chunks * CHUNK, vw), BF16),
                   jax.ShapeDtypeStruct((n_seq, heads, dk, dv), F32)),
        grid=(n_seq, n_chunks),
        in_specs=[pl.BlockSpec((CHUNK, qk), rowmap(0)),
                  pl.BlockSpec((CHUNK, qk), rowmap(1)),
                  pl.BlockSpec((CHUNK, vw), rowmap(1)),
                  pl.BlockSpec((CHUNK, vw), rowmap(2)),
                  pl.BlockSpec((CHUNK, dk // 2), lambda b, c: (c, 0)),
                  pl.BlockSpec((CHUNK, dk // 2), lambda b, c: (c, 0)),
                  pl.BlockSpec((heads, dk, dv), lambda b, c: (0, 0, 0))],
        out_specs=(pl.BlockSpec((CHUNK, vw), lambda b, c: (b * n_chunks + c, 0)),
                   pl.BlockSpec((1, heads, dk, dv), lambda b, c: (b, 0, 0, 0))),
        scratch_shapes=[pltpu.VMEM((heads, dk, dv), F32)],
        compiler_params=_cparams("parallel", "arbitrary"),
        name="ret_chunks",
    )(proj, proj, proj, proj, cos, sin, s0)


def _ret_step_kernel(q_ref, k_ref, v_ref, gr_ref, cos_ref, sin_ref, s_ref, og_ref, sout_ref,
                     *, heads, dk, dv, nb):
    half = dk // 2
    cos = cos_ref[...]
    sin = sin_ref[...]
    base = pl.program_id(0) * nb
    pad_rows = 16
    first = lax.broadcasted_iota(jnp.int32, (pad_rows, 1), 0) == 0

    def rope(x):
        x1 = x[:, :half]
        x2 = x[:, half:]
        return jnp.concatenate([x1 * cos - x2 * sin, x2 * cos + x1 * sin], axis=1)

    for i in range(nb):
        n = base + i
        q_row, k_row, v_row, g_row = (ref[pl.ds(n, 1), :] for ref in (q_ref, k_ref, v_ref, gr_ref))
        o_parts = []
        for h in range(heads):
            gamma = 1.0 - 2.0 ** (-5.0 - h)
            qs = slice(h * dk, (h + 1) * dk)
            vs = slice(h * dv, (h + 1) * dv)
            q = rope(q_row[:, qs])
            k = rope(k_row[:, qs]) * (dk ** -0.5)
            v = v_row[:, vs]
            k_rows = jnp.where(first, jnp.broadcast_to(k, (pad_rows, dk)), 0.0)
            v_rows = jnp.broadcast_to(v, (pad_rows, dv))
            s_new = gamma * s_ref[i, h] + _dot_tn(_bf(k_rows), _bf(v_rows))
            sout_ref[i, h] = s_new
            o = _dot(_bf(jnp.broadcast_to(q, (pad_rows, dk))), _bf(s_new))[0:1]
            o = o * lax.rsqrt(jnp.mean(o * o, axis=-1, keepdims=True) + NORM_EPS)
            g = g_row[:, vs]
            o_parts.append(o * (g * _sigmoid(g)))
        og_ref[pl.ds(n, 1), :] = jnp.concatenate(o_parts, axis=1)


def _ret_step(q, k, v, gr, cos, sin, state, *, heads, dk, dv, nb):
    n_seq = q.shape[0]
    full = lambda a: pl.BlockSpec(a.shape, lambda i: (0,) * a.ndim)
    kern = functools.partial(_ret_step_kernel, heads=heads, dk=dk, dv=dv, nb=nb)
    st_spec = pl.BlockSpec((nb, heads, dk, dv), lambda i: (i, 0, 0, 0))
    return pl.pallas_call(
        kern,
        out_shape=(jax.ShapeDtypeStruct((n_seq, heads * dv), F32),
                   jax.ShapeDtypeStruct(state.shape, F32)),
        grid=(n_seq // nb,),
        in_specs=[full(q), full(k), full(v), full(gr), full(cos), full(sin), st_spec],
        out_specs=(pl.BlockSpec((n_seq, heads * dv), lambda i: (0, 0)), st_spec),
        compiler_params=_cparams("arbitrary"),
        name="ret_step",
    )(q, k, v, gr, cos, sin, state)


def _rwkv_prep_kernel(*refs, layer0, full_prev, tm, rc, hd, lw_, la_, seq_tiles):
    if full_prev:
        (rw_ref, prev_ref, mu_ref, w0_ref, w2_ref, a0_ref, a2_ref, g2_ref, kkp_ref, kap_ref,
         v0_ref, v1_ref, v2_ref, vf_ref, r_o, lw_o, k_o, v_o, kk_o, b_o, g_o) = refs
    else:
        (rw_ref, tail_ref, start_ref, mu_ref, w0_ref, w2_ref, a0_ref, a2_ref, g2_ref, kkp_ref, kap_ref,
         v0_ref, v1_ref, v2_ref, vf_ref, r_o, lw_o, k_o, v_o, kk_o, b_o, g_o) = refs
        i = pl.program_id(0)
        is_start = (i % seq_tiles) == 0
        row0 = lax.broadcasted_iota(jnp.int32, (tm, 1), 0) == 0

    def mixed(sl):
        cur = rw_ref[:, sl].astype(F32)
        if full_prev:
            prev = prev_ref[:, sl]
        else:
            first = jnp.where(is_start, start_ref[:, sl], tail_ref[15:16, sl].astype(F32))
            prev = jnp.where(row0, first, pltpu.roll(cur, 1, 0))
        return cur + (prev - cur) * mu_ref[:, sl]

    z_l = mixed(slice(3 * rc, rw_ref.shape[1]))
    wd = z_l[:, :lw_]
    ad = z_l[:, lw_:lw_ + la_]
    gd = z_l[:, lw_ + la_:]
    w_in = w0_ref[...] + _dot(_bf(jnp.tanh(wd)), w2_ref[...])
    neg = -w_in
    softplus = jnp.maximum(neg, 0.0) + jnp.log(1.0 + jnp.exp(-jnp.abs(neg)))
    lw_o[...] = -jnp.exp(-softplus - 0.5)
    a = _sigmoid(a0_ref[...] + _dot(_bf(ad), a2_ref[...]))
    g_o[...] = _bf(_dot(_bf(_sigmoid(gd)), g2_ref[...]))

    r_o[...] = _bf(mixed(slice(0, rc)))

    z_k = mixed(slice(rc, 2 * rc))
    kk = z_k * kkp_ref[...]
    ones = _group_ones(hd)
    norm = jnp.maximum(jnp.sqrt(_group_sum(kk * kk, ones)), KK_EPS)
    kk = kk / norm
    kk_o[...] = _bf(kk)
    b_o[...] = _bf(kk * a)
    k_o[...] = _bf(z_k * (1.0 + (a - 1.0) * kap_ref[...]))

    z_v = mixed(slice(2 * rc, 3 * rc))
    if layer0:
        v_o[...] = _bf(z_v)
    else:
        lora = _dot(_bf(_dot(_bf(z_v), v1_ref[...])), v2_ref[...])
        v_o[...] = _bf(z_v + (vf_ref[...].astype(F32) - z_v) * _sigmoid(v0_ref[...] + lora))


def _rwkv_prep(rw, prev, start_row, p, v_first, *, layer0, tm, seq_rows, hd):
    rows, width = rw.shape
    rc = p["w0"].shape[1]
    lw_ = p["w2"].shape[0]
    la_ = p["a2"].shape[0]
    full_prev = prev is not None
    tile = lambda w: pl.BlockSpec((tm, w), lambda i: (i, 0))
    const = lambda a: pl.BlockSpec(a.shape, lambda i: (0,) * a.ndim)
    params = [p["mu"], p["w0"], p["w2"], p["a0"], p["a2"], p["g2"], p["kk"], p["ka"], p["v0"], p["v1"], p["v2"]]
    if full_prev:
        lead, lead_specs = [rw, prev], [tile(width), tile(width)]
    else:
        assert tm % 16 == 0 and seq_rows % tm == 0
        tail_spec = pl.BlockSpec((16, width), lambda i: (jnp.maximum(i * (tm // 16) - 1, 0), 0))
        lead, lead_specs = [rw, rw, start_row], [tile(width), tail_spec, const(start_row)]
    kern = functools.partial(_rwkv_prep_kernel, layer0=layer0, full_prev=full_prev, tm=tm, rc=rc, hd=hd,
                             lw_=lw_, la_=la_, seq_tiles=max(seq_rows // tm, 1))
    out_bf = jax.ShapeDtypeStruct((rows, rc), BF16)
    return pl.pallas_call(
        kern,
        out_shape=(out_bf, jax.ShapeDtypeStruct((rows, rc), F32), out_bf, out_bf, out_bf, out_bf, out_bf),
        grid=(rows // tm,),
        in_specs=lead_specs + [const(a) for a in params] + [tile(rc)],
        out_specs=tuple(tile(rc) for _ in range(7)),
        compiler_params=_cparams("parallel"),
        name="rwkv_prep",
    )(*lead, *params, v_first)


def _wkv_chunk_kernel(r_ref, lw_ref, k_ref, v_ref, kk_ref, b_ref, s0_ref, y_ref, sout_ref, s_scr,
                      *, heads, hd, n_pad):
    c = pl.program_id(1)
    n_chunks = pl.num_programs(1)
    C = CHUNK

    @pl.when(c == 0)
    def _():
        s_scr[...] = s0_ref[...]

    ti = lax.broadcasted_iota(jnp.int32, (C, C), 0)
    tj = lax.broadcasted_iota(jnp.int32, (C, C), 1)
    lower = ti >= tj
    strict = ti > tj
    tril = jnp.where(lower, 1.0, 0.0).astype(BF16)

    lw = lw_ref[...]
    hi = _bf(lw)
    r1 = lw - hi.astype(F32)
    mid = _bf(r1)
    lo = _bf(r1 - mid.astype(F32))
    cum = _dot(tril, hi) + _dot(tril, mid) + _dot(tril, lo)
    e_in = jnp.exp(cum)
    e_inv = jnp.exp(-cum)
    e_prev = jnp.exp(cum - lw)
    total = cum[C - 1:C, :]
    e_tail = jnp.exp(total - cum)
    p_end = jnp.exp(total)

    kk = kk_ref[...].astype(F32)
    bb = b_ref[...].astype(F32)
    kx = k_ref[...].astype(F32)
    vx = v_ref[...]
    if n_pad:
        rowm = lax.broadcasted_iota(jnp.int32, kk.shape, 0) >= n_pad
        kk = jnp.where(rowm, kk, 0.0)
        bb = jnp.where(rowm, bb, 0.0)
        kx = jnp.where(rowm, kx, 0.0)
        vx = jnp.where(rowm, vx, jnp.zeros_like(vx))
    a_t = _bf(-kk * e_prev)
    b_t = _bf(bb * e_inv)
    k_t = _bf(kx * e_inv)
    r_t = _bf(r_ref[...].astype(F32) * e_in)
    b_hat = _bf(bb * e_tail)
    k_hat = _bf(kx * e_tail)

    for h in range(heads):
        hs = slice(h * hd, (h + 1) * hd)
        ah, bh, kh, rh, vh = a_t[:, hs], b_t[:, hs], k_t[:, hs], r_t[:, hs], vx[:, hs]
        s0 = s_scr[h]
        s0b = _bf(s0)
        s4 = _dot_nt(jnp.concatenate([ah, rh], axis=0), jnp.concatenate([bh, kh], axis=0))
        n_ = jnp.where(strict, s4[:C, :C], 0.0)
        a_ak = jnp.where(strict, s4[:C, C:], 0.0)
        a_rb = jnp.where(lower, s4[C:, :C], 0.0)
        a_rk = jnp.where(lower, s4[C:, C:], 0.0)
        x = _dot_nt(ah, s0b) + _dot(_bf(a_ak), vh)
        m = n_
        for _ in range(5):
            mb = _bf(m)
            wide = _dot(mb, jnp.concatenate([mb, _bf(x)], axis=1))
            m = wide[:, :C]
            x = x + wide[:, C:]
        u = x + _dot(_bf(m), _bf(x))
        ub = _bf(u)
        y = _dot_nt(rh, s0b) + _dot(_bf(a_rb), ub) + _dot(_bf(a_rk), vh)
        y_ref[:, hs] = y
        s_scr[h] = s0 * p_end[:, hs] + _dot_tn(ub, b_hat[:, hs]) + _dot_tn(vh, k_hat[:, hs])

    @pl.when(c == n_chunks - 1)
    def _():
        sout_ref[0] = s_scr[...]


def _wkv_chunks(r, lw, k, v, kk, b, s0, *, row_block0, n_seq, n_chunks, heads, hd, n_pad):
    rc = heads * hd
    tile = pl.BlockSpec((CHUNK, rc), lambda i, c: (row_block0 + i * n_chunks + c, 0))
    kern = functools.partial(_wkv_chunk_kernel, heads=heads, hd=hd, n_pad=n_pad)
    return pl.pallas_call(
        kern,
        out_shape=(jax.ShapeDtypeStruct((n_seq * n_chunks * CHUNK, rc), F32),
                   jax.ShapeDtypeStruct((n_seq, heads, hd, hd), F32)),
        grid=(n_seq, n_chunks),
        in_specs=[tile] * 6 + [pl.BlockSpec((heads, hd, hd), lambda i, c: (0, 0, 0))],
        out_specs=(pl.BlockSpec((CHUNK, rc), lambda i, c: (i * n_chunks + c, 0)),
                   pl.BlockSpec((1, heads, hd, hd), lambda i, c: (i, 0, 0, 0))),
        scratch_shapes=[pltpu.VMEM((heads, hd, hd), F32)],
        compiler_params=_cparams("parallel", "arbitrary"),
        name="wkv_chunks",
    )(r, lw, k, v, kk, b, s0)


def _wkv_step_kernel(r_ref, lw_ref, k_ref, v_ref, kk_ref, b_ref, s_ref, y_ref, sout_ref, *, heads, hd, nb):
    base = pl.program_id(0) * nb
    eye = lax.broadcasted_iota(jnp.int32, (hd, hd), 0) == lax.broadcasted_iota(jnp.int32, (hd, hd), 1)

    def body(i, carry):
        n = base + i
        r_row, lw_row, k_row, v_row, kk_row, b_row = (
            ref[pl.ds(n, 1), :] for ref in (r_ref, lw_ref, k_ref, v_ref, kk_ref, b_ref))
        w_row = jnp.exp(lw_row)
        y_parts = []
        for h in range(heads):
            hs = slice(h * hd, (h + 1) * hd)
            s = s_ref[i, h]
            s_kk = jnp.sum(s * kk_row[:, hs], axis=-1, keepdims=True)
            v_col = jnp.sum(jnp.where(eye, jnp.broadcast_to(v_row[:, hs], (hd, hd)), 0.0), axis=-1, keepdims=True)
            s_new = s * w_row[:, hs] - s_kk * b_row[:, hs] + v_col * k_row[:, hs]
            sout_ref[i, h] = s_new
            y_col = jnp.sum(s_new * r_row[:, hs], axis=-1, keepdims=True)
            y_parts.append(jnp.sum(jnp.where(eye, jnp.broadcast_to(y_col, (hd, hd)), 0.0), axis=0, keepdims=True))
        y_ref[pl.ds(n, 1), :] = jnp.concatenate(y_parts, axis=1)
        return carry

    lax.fori_loop(0, nb, body, 0)


def _wkv_step(r, lw, k, v, kk, b, state, *, heads, hd, nb):
    n_seq = r.shape[0]
    full = lambda a: pl.BlockSpec(a.shape, lambda i: (0,) * a.ndim)
    st_spec = pl.BlockSpec((nb, heads, hd, hd), lambda i: (i, 0, 0, 0))
    kern = functools.partial(_wkv_step_kernel, heads=heads, hd=hd, nb=nb)
    return pl.pallas_call(
        kern,
        out_shape=(jax.ShapeDtypeStruct((n_seq, heads * hd), F32), jax.ShapeDtypeStruct(state.shape, F32)),
        grid=(n_seq // nb,),
        in_specs=[full(a) for a in (r, lw, k, v, kk, b)] + [st_spec],
        out_specs=(pl.BlockSpec((n_seq, heads * hd), lambda i: (0, 0)), st_spec),
        compiler_params=_cparams("arbitrary"),
        name="wkv_step",
    )(r, lw, k, v, kk, b, state)


def _mix_kernel(og_ref, y_ref, r_ref, k_ref, v_ref, g_ref, gates_ref, x_ref,
                wret_ref, wrw_ref, wout_ref, lnw_ref, lnb_ref, rk_ref, gain_ref, o_ref, *, hd, d):
    y_ret = _dot(og_ref[...], wret_ref[...])
    ones = _group_ones(hd)
    y = y_ref[...]
    inv_n = 1.0 / hd
    mean = _group_sum(y, ones) * inv_n
    dlt = y - mean
    var = _group_sum(dlt * dlt, ones) * inv_n
    yn = dlt * lax.rsqrt(var + LNX_EPS) * lnw_ref[...] + lnb_ref[...]
    rkk = r_ref[...].astype(F32) * k_ref[...].astype(F32) * rk_ref[...]
    bonus = _group_sum(rkk, ones) * v_ref[...].astype(F32)
    y_rw = _dot(_bf((yn + bonus) * g_ref[...].astype(F32)), wrw_ref[...])
    gates = gates_ref[...].astype(F32)
    mix = _sigmoid(gates[:, :d]) * y_ret + _sigmoid(gates[:, d:]) * y_rw
    o_ref[...] = x_ref[...] + _rms(_dot(_bf(mix), wout_ref[...]), gain_ref[...])


def _mix(og, y, r, k, v, g, proj, x, w_ret, w_rw, w_out, lnw, lnb, rk, gain, *, tm, hd, gate_block):
    rows, d = x.shape
    tile = lambda a: pl.BlockSpec((tm, a.shape[1]), lambda i: (i, 0))
    const = lambda a: pl.BlockSpec(a.shape, lambda i: (0,) * a.ndim)
    kern = functools.partial(_mix_kernel, hd=hd, d=d)
    return pl.pallas_call(
        kern,
        out_shape=jax.ShapeDtypeStruct((rows, d), F32),
        grid=(rows // tm,),
        in_specs=[tile(og), tile(y), tile(r), tile(k), tile(v), tile(g),
                  pl.BlockSpec((tm, 2 * d), lambda i: (i, gate_block)), tile(x),
                  const(w_ret), const(w_rw), const(w_out), const(lnw), const(lnb), const(rk), const(gain)],
        out_specs=pl.BlockSpec((tm, d), lambda i: (i, 0)),
        compiler_params=_cparams("parallel"),
        name="mix",
    )(og, y, r, k, v, g, proj, x, w_ret, w_rw, w_out, lnw, lnb, rk, gain)


def _ffn_kernel(x_ref, g2_ref, g3_ref, w1_ref, w3_ref, w2_ref, o_ref, hn_ref, acc_ref):
    j = pl.program_id(1)

    @pl.when(j == 0)
    def _():
        hn_ref[...] = _bf(_rms(x_ref[...], g2_ref[...]))
        acc_ref[...] = jnp.zeros_like(acc_ref)

    hn = hn_ref[...]
    a = _dot(hn, w1_ref[...])
    h = (a * _sigmoid(a)) * _dot(hn, w3_ref[...])
    acc_ref[...] += _dot(_bf(h), w2_ref[...])

    @pl.when(j == pl.num_programs(1) - 1)
    def _():
        o_ref[...] = x_ref[...] + _rms(acc_ref[...], g3_ref[...])


def _ffn(x, gain2, gain3, w1, w3, w2, *, tm, tf):
    rows, d = x.shape
    f = w1.shape[1]
    return pl.pallas_call(
        _ffn_kernel,
        out_shape=jax.ShapeDtypeStruct((rows, d), F32),
        grid=(rows // tm, f // tf),
        in_specs=[pl.BlockSpec((tm, d), lambda i, j: (i, 0)),
                  pl.BlockSpec((1, d), lambda i, j: (0, 0)),
                  pl.BlockSpec((1, d), lambda i, j: (0, 0)),
                  pl.BlockSpec((d, tf), lambda i, j: (0, j)),
                  pl.BlockSpec((d, tf), lambda i, j: (0, j)),
                  pl.BlockSpec((tf, d), lambda i, j: (j, 0))],
        out_specs=pl.BlockSpec((tm, d), lambda i, j: (i, 0)),
        scratch_shapes=[pltpu.VMEM((tm, d), BF16), pltpu.VMEM((tm, d), F32)],
        compiler_params=_cparams("parallel", "arbitrary"),
        name="ffn",
    )(x, gain2, gain3, w1, w3, w2)


def _pick(n, prefs):
    for t in prefs:
        if n % t == 0:
            return t
    return n


def _rope_tables(pos, half):
    inv_freq = 1.0 / (ROPE_BASE ** (jnp.arange(half, dtype=F32) / half))
    ang = pos.astype(F32)[:, None] * inv_freq[None, :]
    return jnp.cos(ang), jnp.sin(ang)


def kernel(x_prompt, x_sample, state_ret, state_wkv, state_shift, meta_tokens, norm_gain, w_in, w_ret_out,
           w_rwkv_out, w_out, rwkv_mu, rwkv_w0, rwkv_w2, rwkv_a0, rwkv_a2, rwkv_g2, rwkv_kk, rwkv_ka, rwkv_rk,
           rwkv_lnx_w, rwkv_lnx_b, rwkv_v0, rwkv_v1, rwkv_v2, ffn_w1, ffn_w3, ffn_w2):
    bp, sp, d = x_prompt.shape
    ns = x_sample.shape[0]
    depth = w_in.shape[0]
    n_meta = meta_tokens.shape[0]
    _, _, rh, dk, dv = state_ret.shape
    _, _, wh, hd, _ = state_wkv.shape
    qk, vw, rc = rh * dk, rh * dv, wh * hd
    rw_w = state_shift.shape[-1]
    assert x_sample.shape[1] == 1 and n_meta <= CHUNK and sp % CHUNK == 0
    assert vw == 2 * qk and d == qk and rc == d and hd & (hd - 1) == 0 and MXU_TILE % hd == 0
    n_chunks = sp // CHUNK
    n_pad = CHUNK - n_meta
    main_rows = bp * sp
    small_rows = CHUNK + ns
    half = dk // 2

    xm = x_prompt.reshape(main_rows, d)
    xs = jnp.concatenate([jnp.zeros((n_pad, d), F32), meta_tokens.astype(F32), x_sample.reshape(ns, d)], axis=0)

    cos_m, sin_m = _rope_tables(n_meta + jnp.arange(sp), half)
    cos_t, sin_t = _rope_tables(jnp.arange(CHUNK) - n_pad, half)
    cos_s, sin_s = _rope_tables(jnp.full((1,), PAST_LEN), half)

    split = 2 * qk + 2 * vw
    wa = _bf(jnp.concatenate([w_in[:, :, :split], w_in[:, :, split + rw_w:]], axis=-1))
    wb = _bf(w_in[:, :, split:split + rw_w])
    gate_block = split // (2 * d)
    assert gate_block * 2 * d == split

    tm_m = _pick(main_rows, (512, 256, 128, 64))
    tm_mix = _pick(main_rows, (256, 128, 64))
    tn_a = _pick(wa.shape[-1], (2048, 1024, 512, 256))
    tf = _pick(ffn_w1.shape[-1], (1408, 1024, 768, 512, 384, 256, 128))
    seq_tm = _pick(sp, PREP_TILES)

    zero_row = jnp.zeros((1, rw_w), F32)
    ret_p, wkv_p, sh_p, ret_s, wkv_s, sh_s = [], [], [], [], [], []
    vf_m = vf_s = None
    for l in range(depth):
        g = norm_gain[l]
        p = dict(mu=rwkv_mu[l][None], w0=rwkv_w0[l][None], w2=_bf(rwkv_w2[l]), a0=rwkv_a0[l][None],
                 a2=_bf(rwkv_a2[l]), g2=_bf(rwkv_g2[l]), kk=rwkv_kk[l][None], ka=rwkv_ka[l][None])
        lv = max(l - 1, 0)
        p.update(v0=rwkv_v0[lv][None], v1=_bf(rwkv_v1[lv]), v2=_bf(rwkv_v2[lv]))

        pa_s = _norm_proj(xs, g[0][None], wa[l], small_rows, tn_a)
        pb_s = _norm_proj(xs, g[0][None], wb[l], small_rows, rw_w)
        pb_sf = pb_s.astype(F32)
        prev_s = jnp.concatenate([jnp.zeros((n_pad + 1, rw_w), F32), pb_sf[n_pad:CHUNK - 1], state_shift[l]], axis=0)
        if l == 0:
            vf_s = jnp.zeros((small_rows, rc), BF16)
        r_s, lw_s, k_s, v_s, kk_s, b_s, g_s = _rwkv_prep(pb_s, prev_s, None, p, vf_s, layer0=(l == 0),
                                                         tm=small_rows, seq_rows=small_rows, hd=hd)
        if l == 0:
            vf_s = v_s

        og_meta, s_ret_meta = _ret_chunks(pa_s, cos_t, sin_t, jnp.zeros((rh, dk, dv), F32), row_block0=0,
                                          n_seq=1, n_chunks=1, heads=rh, dk=dk, dv=dv, n_pad=n_pad)
        y_meta, s_wkv_meta = _wkv_chunks(r_s, lw_s, k_s, v_s, kk_s, b_s, jnp.zeros((wh, hd, hd), F32),
                                         row_block0=0, n_seq=1, n_chunks=1, heads=wh, hd=hd, n_pad=n_pad)
        pa_sf = pa_s[CHUNK:].astype(F32)
        og_smp, s_ret_new = _ret_step(pa_sf[:, :qk], pa_sf[:, qk:2 * qk], pa_sf[:, 2 * qk:2 * qk + vw],
                                      pa_sf[:, 2 * qk + vw:split], cos_s, sin_s, state_ret[l],
                                      heads=rh, dk=dk, dv=dv, nb=_pick(ns, (2, 1)))
        smp = [a[CHUNK:].astype(F32) for a in (r_s, lw_s, k_s, v_s, kk_s, b_s)]
        y_smp, s_wkv_new = _wkv_step(*smp, state_wkv[l], heads=wh, hd=hd, nb=_pick(ns, (8, 4, 2, 1)))
        og_s = jnp.concatenate([og_meta, _bf(og_smp)], axis=0)
        y_s = jnp.concatenate([y_meta, y_smp], axis=0)
        ret_s.append(s_ret_new)
        wkv_s.append(s_wkv_new)
        sh_s.append(pb_sf[CHUNK:])

        pa_m = _norm_proj(xm, g[0][None], wa[l], tm_m, tn_a)
        pb_m = _norm_proj(xm, g[0][None], wb[l], tm_m, rw_w)
        if l == 0:
            vf_m = jnp.zeros((main_rows, rc), BF16)
        r_m, lw_m, k_m, v_m, kk_m, b_m, g_m = _rwkv_prep(pb_m, None, pb_sf[CHUNK - 1:CHUNK], p, vf_m,
                                                         layer0=(l == 0), tm=seq_tm, seq_rows=sp, hd=hd)
        if l == 0:
            vf_m = v_m
        og_m, s_ret_m = _ret_chunks(pa_m, cos_m, sin_m, s_ret_meta[0], row_block0=0, n_seq=bp,
                                    n_chunks=n_chunks, heads=rh, dk=dk, dv=dv, n_pad=0)
        y_m, s_wkv_m = _wkv_chunks(r_m, lw_m, k_m, v_m, kk_m, b_m, s_wkv_meta[0], row_block0=0, n_seq=bp,
                                   n_chunks=n_chunks, heads=wh, hd=hd, n_pad=0)
        ret_p.append(s_ret_m)
        wkv_p.append(s_wkv_m)
        sh_p.append(pb_m.reshape(bp, sp, rw_w)[:, -1].astype(F32))

        mixw = (_bf(w_ret_out[l]), _bf(w_rwkv_out[l]), _bf(w_out[l]), rwkv_lnx_w[l][None], rwkv_lnx_b[l][None],
                rwkv_rk[l][None], g[1][None])
        ffnw = (g[2][None], g[3][None], _bf(ffn_w1[l]), _bf(ffn_w3[l]), _bf(ffn_w2[l]))
        xs = _mix(og_s, y_s, r_s, k_s, v_s, g_s, pa_s, xs, *mixw, tm=small_rows, hd=hd, gate_block=gate_block)
        xs = _ffn(xs, *ffnw, tm=small_rows, tf=tf)
        xm = _mix(og_m, y_m, r_m, k_m, v_m, g_m, pa_m, xm, *mixw, tm=tm_mix, hd=hd, gate_block=gate_block)
        xm = _ffn(xm, *ffnw, tm=tm_m, tf=tf)

    return (xm.reshape(bp, sp, d), xs[CHUNK:].reshape(ns, 1, d), jnp.stack(ret_p), jnp.stack(wkv_p),
            jnp.stack(sh_p), jnp.stack(ret_s), jnp.stack(wkv_s), jnp.stack(sh_s))
```

```python
import functools
import math

import jax
import jax.numpy as jnp
from jax import lax
from jax.experimental import pallas as pl
from jax.experimental.pallas import tpu as pltpu

F32 = jnp.float32
BF16 = jnp.bfloat16

NORM_EPS = 1e-6
LNX_EPS = 64e-5
ROPE_BASE = 10000.0
PAST_LEN = 16384
KK_EPS = 1e-12

CHUNK = 64
MXU_TILE = 256
VMEM_LIMIT = 56 * 1024 * 1024
PREP_TILES = (512, 256, 128, 64)


def _cparams(*sem):
    return pltpu.CompilerParams(dimension_semantics=sem, vmem_limit_bytes=VMEM_LIMIT)


def _dot(a, b):
    return jnp.dot(a, b, preferred_element_type=F32)


def _dot_nt(a, b):
    return lax.dot_general(a, b, (((1,), (1,)), ((), ())), preferred_element_type=F32)


def _dot_tn(a, b):
    return lax.dot_general(a, b, (((0,), (0,)), ((), ())), preferred_element_type=F32)


def _bf(x):
    return x.astype(BF16)


def _sigmoid(x):
    return 1.0 / (1.0 + jnp.exp(-x))


def _rms(x, gain):
    return x * lax.rsqrt(jnp.mean(x * x, axis=-1, keepdims=True) + NORM_EPS) * gain


def _group_ones(group):
    shift = group.bit_length() - 1
    r = lax.broadcasted_iota(jnp.int32, (MXU_TILE, MXU_TILE), 0) >> shift
    c = lax.broadcasted_iota(jnp.int32, (MXU_TILE, MXU_TILE), 1) >> shift
    return jnp.where(r == c, 1.0, 0.0).astype(BF16)


def _group_sum(x, ones):
    hi = _bf(x)
    lo = _bf(x - hi.astype(F32))
    parts = []
    for j in range(x.shape[1] // MXU_TILE):
        sl = slice(j * MXU_TILE, (j + 1) * MXU_TILE)
        parts.append(_dot(hi[:, sl], ones) + _dot(lo[:, sl], ones))
    return parts[0] if len(parts) == 1 else jnp.concatenate(parts, axis=1)


def _norm_proj_kernel(x_ref, g_ref, w_ref, o_ref, xn_ref):
    @pl.when(pl.program_id(1) == 0)
    def _():
        xn_ref[...] = _bf(_rms(x_ref[...], g_ref[...]))

    o_ref[...] = _dot(xn_ref[...], w_ref[...]).astype(o_ref.dtype)


def _norm_proj(x, gain, w, tm, tn):
    rows, d = x.shape
    n = w.shape[1]
    return pl.pallas_call(
        _norm_proj_kernel,
        out_shape=jax.ShapeDtypeStruct((rows, n), BF16),
        grid=(rows // tm, n // tn),
        in_specs=[pl.BlockSpec((tm, d), lambda i, j: (i, 0)),
                  pl.BlockSpec((1, d), lambda i, j: (0, 0)),
                  pl.BlockSpec((d, tn), lambda i, j: (0, j))],
        out_specs=pl.BlockSpec((tm, tn), lambda i, j: (i, j)),
        scratch_shapes=[pltpu.VMEM((tm, d), BF16)],
        compiler_params=_cparams("parallel", "arbitrary"),
        name="norm_proj",
    )(x, gain, w)


def _ret_chunk_kernel(q_ref, k_ref, v_ref, gr_ref, cos_ref, sin_ref, s0_ref, og_ref, sout_ref, s_scr,
                      *, heads, dk, dv, n_pad):
    c = pl.program_id(1)
    n_chunks = pl.num_programs(1)
    half = dk // 2

    @pl.when(c == 0)
    def _():
        s_scr[...] = s0_ref[...]

    cos = cos_ref[...]
    sin = sin_ref[...]
    row = lax.broadcasted_iota(jnp.int32, (CHUNK, half), 0)
    rowf = row.astype(F32)
    rel_i = lax.broadcasted_iota(jnp.int32, (CHUNK, CHUNK), 0)
    rel_j = lax.broadcasted_iota(jnp.int32, (CHUNK, CHUNK), 1)
    rel = (rel_i - rel_j).astype(F32)

    def rope(x):
        x1 = x[:, :half]
        x2 = x[:, half:]
        return x1 * cos - x2 * sin, x2 * cos + x1 * sin

    hr = range(heads)
    lgs = [math.log(1.0 - 2.0 ** (-5.0 - h)) for h in hr]
    vsl = [slice(h * dv, (h + 1) * dv) for h in hr]
    qb, kb, qd, kd, vhs, dmask = [], [], [], [], [], []
    for h in hr:
        lg = lgs[h]
        qs = slice(h * dk, (h + 1) * dk)
        q1, q2 = rope(q_ref[:, qs].astype(F32))
        k1, k2 = rope(k_ref[:, qs].astype(F32))
        k1 = k1 * (dk ** -0.5)
        k2 = k2 * (dk ** -0.5)
        vh = v_ref[:, vsl[h]]
        if n_pad:
            k1 = jnp.where(row >= n_pad, k1, 0.0)
            k2 = jnp.where(row >= n_pad, k2, 0.0)
            vrow = lax.broadcasted_iota(jnp.int32, (CHUNK, dv), 0)
            vh = jnp.where(vrow >= n_pad, vh, jnp.zeros_like(vh))
        q_decay = jnp.exp((rowf + 1.0) * lg)
        k_decay = jnp.exp((CHUNK - 1.0 - rowf) * lg)
        dmask.append(jnp.where(rel >= 0, jnp.exp(lg * jnp.maximum(rel, 0.0)), 0.0))
        qb.append(_bf(jnp.concatenate([q1, q2], axis=1)))
        kb.append(_bf(jnp.concatenate([k1, k2], axis=1)))
        qd.append(_bf(jnp.concatenate([q1 * q_decay, q2 * q_decay], axis=1)))
        kd.append(_bf(jnp.concatenate([k1 * k_decay, k2 * k_decay], axis=1)))
        vhs.append(vh)
    s_old = [s_scr[h] for h in hr]
    scores = [_bf(_dot_nt(qb[h], kb[h]) * dmask[h]) for h in hr]
    cross = [_dot(qd[h], _bf(s_old[h])) for h in hr]
    inner = [_dot(scores[h], vhs[h]) for h in hr]
    for h in hr:
        o = inner[h] + cross[h]
        o = o * lax.rsqrt(jnp.mean(o * o, axis=-1, keepdims=True) + NORM_EPS)
        g = gr_ref[:, vsl[h]].astype(F32)
        og_ref[:, vsl[h]] = _bf(o * (g * _sigmoid(g)))
    for h in hr:
        s_scr[h] = math.exp(CHUNK * lgs[h]) * s_old[h] + _dot_tn(kd[h], vhs[h])

    @pl.when(c == n_chunks - 1)
    def _():
        sout_ref[0] = s_scr[...]


def _ret_chunks(proj, cos, sin, s0, *, row_block0, n_seq, n_chunks, heads, dk, dv, n_pad):
    qk = heads * dk
    vw = heads * dv
    assert vw == 2 * qk

    def rowmap(col):
        return lambda b, c: (row_block0 + b * n_chunks + c, col)

    kern = functools.partial(_ret_chunk_kernel, heads=heads, dk=dk, dv=dv, n_pad=n_pad)
    return pl.pallas_call(
        kern,
        out_shape=(jax.ShapeDtypeStruct((n_seq * n_chunks * CHUNK, vw), BF16),
                   jax.ShapeDtypeStruct((n_seq, heads, dk, dv), F32)),
        grid=(n_seq, n_chunks),
        in_specs=[pl.BlockSpec((CHUNK, qk), rowmap(0)),
                  pl.BlockSpec((CHUNK, qk), rowmap(1)),
                  pl.BlockSpec((CHUNK, vw), rowmap(1)),
                  pl.BlockSpec((CHUNK, vw), rowmap(2)),
                  pl.BlockSpec((CHUNK, dk // 2), lambda b, c: (c, 0)),
                  pl.BlockSpec((CHUNK, dk // 2), lambda b, c: (c, 0)),
                  pl.BlockSpec((heads, dk, dv), lambda b, c: (0, 0, 0))],
        out_specs=(pl.BlockSpec((CHUNK, vw), lambda b, c: (b * n_chunks + c, 0)),
                   pl.BlockSpec((1, heads, dk, dv), lambda b, c: (b, 0, 0, 0))),
        scratch_shapes=[pltpu.VMEM((heads, dk, dv), F32)],
        compiler_params=_cparams("parallel", "arbitrary"),
        name="ret_chunks",
    )(proj, proj, proj, proj, cos, sin, s0)


def _ret_step_kernel(q_ref, k_ref, v_ref, gr_ref, cos_ref, sin_ref, s_ref, *rest, heads, dk, dv, nb):
    og_ref, sout_ref = rest[-2:]
    half = dk // 2
    cos = cos_ref[...]
    sin = sin_ref[...]
    base = pl.program_id(0) * nb
    pad_rows = 16
    first = lax.broadcasted_iota(jnp.int32, (pad_rows, 1), 0) == 0

    def rope(x):
        x1 = x[:, :half]
        x2 = x[:, half:]
        return jnp.concatenate([x1 * cos - x2 * sin, x2 * cos + x1 * sin], axis=1)

    for i in range(nb):
        n = base + i
        q_row, k_row, v_row, g_row = (ref[pl.ds(n, 1), :] for ref in (q_ref, k_ref, v_ref, gr_ref))
        o_parts = []
        for h in range(heads):
            gamma = 1.0 - 2.0 ** (-5.0 - h)
            qs = slice(h * dk, (h + 1) * dk)
            vs = slice(h * dv, (h + 1) * dv)
            q = rope(q_row[:, qs])
            k = rope(k_row[:, qs]) * (dk ** -0.5)
            v = v_row[:, vs]
            k_rows = jnp.where(first, jnp.broadcast_to(k, (pad_rows, dk)), 0.0)
            v_rows = jnp.broadcast_to(v, (pad_rows, dv))
            s_new = gamma * s_ref[0, i, h] + _dot_tn(_bf(k_rows), _bf(v_rows))
            sout_ref[0, i, h] = s_new
            o = _dot(_bf(jnp.broadcast_to(q, (pad_rows, dk))), _bf(s_new))[0:1]
            o = o * lax.rsqrt(jnp.mean(o * o, axis=-1, keepdims=True) + NORM_EPS)
            g = g_row[:, vs]
            o_parts.append(o * (g * _sigmoid(g)))
        og_ref[pl.ds(n, 1), :] = jnp.concatenate(o_parts, axis=1)


def _stacked_alias(stacked_out, n_inputs):
    if stacked_out is None:
        return [], [], {}
    return [stacked_out], [pl.BlockSpec(memory_space=pl.ANY)], {n_inputs: 1}


def _ret_step(q, k, v, gr, cos, sin, states, stacked_out, layer, *, heads, dk, dv, nb):
    n_seq = q.shape[0]
    full = lambda a: pl.BlockSpec(a.shape, lambda i: (0,) * a.ndim)
    kern = functools.partial(_ret_step_kernel, heads=heads, dk=dk, dv=dv, nb=nb)
    st_spec = pl.BlockSpec((1, nb, heads, dk, dv), lambda i: (layer, i, 0, 0, 0))
    extra, extra_specs, aliases = _stacked_alias(stacked_out, 7)
    return pl.pallas_call(
        kern,
        out_shape=(jax.ShapeDtypeStruct((n_seq, heads * dv), F32),
                   jax.ShapeDtypeStruct(states.shape, F32)),
        grid=(n_seq // nb,),
        in_specs=[full(q), full(k), full(v), full(gr), full(cos), full(sin), st_spec] + extra_specs,
        out_specs=(pl.BlockSpec((n_seq, heads * dv), lambda i: (0, 0)), st_spec),
        input_output_aliases=aliases,
        compiler_params=_cparams("arbitrary"),
        name="ret_step",
    )(q, k, v, gr, cos, sin, states, *extra)


def _rwkv_prep_kernel(*refs, layer0, full_prev, tm, rc, hd, lw_, la_, seq_tiles):
    if full_prev:
        (rw_ref, prev_ref, mu_ref, w0_ref, w2_ref, a0_ref, a2_ref, g2_ref, kkp_ref, kap_ref,
         v0_ref, v1_ref, v2_ref, vf_ref, r_o, lw_o, k_o, v_o, kk_o, b_o, g_o) = refs
    else:
        (rw_ref, tail_ref, start_ref, mu_ref, w0_ref, w2_ref, a0_ref, a2_ref, g2_ref, kkp_ref, kap_ref,
         v0_ref, v1_ref, v2_ref, vf_ref, r_o, lw_o, k_o, v_o, kk_o, b_o, g_o) = refs
        i = pl.program_id(0)
        is_start = (i % seq_tiles) == 0
        row0 = lax.broadcasted_iota(jnp.int32, (tm, 1), 0) == 0

    def mixed(sl):
        cur = rw_ref[:, sl].astype(F32)
        if full_prev:
            prev = prev_ref[:, sl]
        else:
            first = jnp.where(is_start, start_ref[:, sl], tail_ref[15:16, sl].astype(F32))
            prev = jnp.where(row0, first, pltpu.roll(cur, 1, 0))
        return cur + (prev - cur) * mu_ref[:, sl]

    z_l = mixed(slice(3 * rc, rw_ref.shape[1]))
    wd = z_l[:, :lw_]
    ad = z_l[:, lw_:lw_ + la_]
    gd = z_l[:, lw_ + la_:]
    w_in = w0_ref[...] + _dot(_bf(jnp.tanh(wd)), w2_ref[...])
    neg = -w_in
    softplus = jnp.maximum(neg, 0.0) + jnp.log(1.0 + jnp.exp(-jnp.abs(neg)))
    lw_o[...] = -jnp.exp(-softplus - 0.5)
    a = _sigmoid(a0_ref[...] + _dot(_bf(ad), a2_ref[...]))
    g_o[...] = _bf(_dot(_bf(_sigmoid(gd)), g2_ref[...]))

    r_o[...] = _bf(mixed(slice(0, rc)))

    z_k = mixed(slice(rc, 2 * rc))
    kk = z_k * kkp_ref[...]
    ones = _group_ones(hd)
    norm = jnp.maximum(jnp.sqrt(_group_sum(kk * kk, ones)), KK_EPS)
    kk = kk / norm
    kk_o[...] = _bf(kk)
    b_o[...] = _bf(kk * a)
    k_o[...] = _bf(z_k * (1.0 + (a - 1.0) * kap_ref[...]))

    z_v = mixed(slice(2 * rc, 3 * rc))
    if layer0:
        v_o[...] = _bf(z_v)
    else:
        lora = _dot(_bf(_dot(_bf(z_v), v1_ref[...])), v2_ref[...])
        v_o[...] = _bf(z_v + (vf_ref[...].astype(F32) - z_v) * _sigmoid(v0_ref[...] + lora))


def _rwkv_prep(rw, prev, start_row, p, v_first, *, layer0, tm, seq_rows, hd):
    rows, width = rw.shape
    rc = p["w0"].shape[1]
    lw_ = p["w2"].shape[0]
    la_ = p["a2"].shape[0]
    full_prev = prev is not None
    tile = lambda w: pl.BlockSpec((tm, w), lambda i: (i, 0))
    const = lambda a: pl.BlockSpec(a.shape, lambda i: (0,) * a.ndim)
    params = [p["mu"], p["w0"], p["w2"], p["a0"], p["a2"], p["g2"], p["kk"], p["ka"], p["v0"], p["v1"], p["v2"]]
    if full_prev:
        lead, lead_specs = [rw, prev], [tile(width), tile(width)]
    else:
        assert tm % 16 == 0 and seq_rows % tm == 0
        tail_spec = pl.BlockSpec((16, width), lambda i: (jnp.maximum(i * (tm // 16) - 1, 0), 0))
        lead, lead_specs = [rw, rw, start_row], [tile(width), tail_spec, const(start_row)]
    kern = functools.partial(_rwkv_prep_kernel, layer0=layer0, full_prev=full_prev, tm=tm, rc=rc, hd=hd,
                             lw_=lw_, la_=la_, seq_tiles=max(seq_rows // tm, 1))
    out_bf = jax.ShapeDtypeStruct((rows, rc), BF16)
    return pl.pallas_call(
        kern,
        out_shape=(out_bf, jax.ShapeDtypeStruct((rows, rc), F32), out_bf, out_bf, out_bf, out_bf, out_bf),
        grid=(rows // tm,),
        in_specs=lead_specs + [const(a) for a in params] + [tile(rc)],
        out_specs=tuple(tile(rc) for _ in range(7)),
        compiler_params=_cparams("parallel"),
        name="rwkv_prep",
    )(*lead, *params, v_first)


def _wkv_chunk_kernel(r_ref, lw_ref, k_ref, v_ref, kk_ref, b_ref, s0_ref, y_ref, sout_ref, s_scr,
                      *, heads, hd, n_pad):
    c = pl.program_id(1)
    n_chunks = pl.num_programs(1)
    C = CHUNK

    @pl.when(c == 0)
    def _():
        s_scr[...] = s0_ref[...]

    ti = lax.broadcasted_iota(jnp.int32, (C, C), 0)
    tj = lax.broadcasted_iota(jnp.int32, (C, C), 1)
    lower = ti >= tj
    strict = ti > tj
    tril = jnp.where(lower, 1.0, 0.0).astype(BF16)

    lw = lw_ref[...]
    hi = _bf(lw)
    r1 = lw - hi.astype(F32)
    mid = _bf(r1)
    lo = _bf(r1 - mid.astype(F32))
    cum = _dot(tril, hi) + _dot(tril, mid) + _dot(tril, lo)
    e_in = jnp.exp(cum)
    e_inv = jnp.exp(-cum)
    e_prev = jnp.exp(cum - lw)
    total = cum[C - 1:C, :]
    e_tail = jnp.exp(total - cum)
    p_end = jnp.exp(total)

    kk = kk_ref[...].astype(F32)
    bb = b_ref[...].astype(F32)
    kx = k_ref[...].astype(F32)
    vx = v_ref[...]
    if n_pad:
        rowm = lax.broadcasted_iota(jnp.int32, kk.shape, 0) >= n_pad
        kk = jnp.where(rowm, kk, 0.0)
        bb = jnp.where(rowm, bb, 0.0)
        kx = jnp.where(rowm, kx, 0.0)
        vx = jnp.where(rowm, vx, jnp.zeros_like(vx))
    a_t = _bf(-kk * e_prev)
    b_t = _bf(bb * e_inv)
    k_t = _bf(kx * e_inv)
    r_t = _bf(r_ref[...].astype(F32) * e_in)
    b_hat = _bf(bb * e_tail)
    k_hat = _bf(kx * e_tail)

    hr = range(heads)
    hs = [slice(h * hd, (h + 1) * hd) for h in hr]
    ti2 = lax.broadcasted_iota(jnp.int32, (C, 2 * C), 0)
    tj2 = lax.broadcasted_iota(jnp.int32, (C, 2 * C), 1) & (C - 1)
    strict2 = ti2 > tj2
    lower2 = ti2 >= tj2
    s0 = [s_scr[h] for h in hr]
    s0b = [_bf(s) for s in s0]
    s4 = [_dot_nt(jnp.concatenate([a_t[:, hs[h]], r_t[:, hs[h]]], axis=0),
                  jnp.concatenate([b_t[:, hs[h]], k_t[:, hs[h]]], axis=0)) for h in hr]
    top = [jnp.where(strict2, s[:C], 0.0) for s in s4]
    bot = [_bf(jnp.where(lower2, s[C:], 0.0)) for s in s4]
    m = [t[:, :C] for t in top]
    x = [_dot_nt(a_t[:, hs[h]], s0b[h]) + _dot(_bf(top[h][:, C:]), vx[:, hs[h]]) for h in hr]
    for _ in range(5):
        mb = [_bf(v) for v in m]
        wide = [_dot(mb[h], jnp.concatenate([mb[h], _bf(x[h])], axis=1)) for h in hr]
        m = [w[:, :C] for w in wide]
        x = [x[h] + wide[h][:, C:] for h in hr]
    u = [x[h] + _dot(_bf(m[h]), _bf(x[h])) for h in hr]
    uv = [jnp.concatenate([_bf(u[h]), vx[:, hs[h]]], axis=0) for h in hr]
    for h in hr:
        y_ref[:, hs[h]] = _dot_nt(r_t[:, hs[h]], s0b[h]) + _dot(bot[h], uv[h])
    for h in hr:
        s_scr[h] = s0[h] * p_end[:, hs[h]] + _dot_tn(
            uv[h], jnp.concatenate([b_hat[:, hs[h]], k_hat[:, hs[h]]], axis=0))

    @pl.when(c == n_chunks - 1)
    def _():
        sout_ref[0] = s_scr[...]


def _wkv_chunks(r, lw, k, v, kk, b, s0, *, row_block0, n_seq, n_chunks, heads, hd, n_pad):
    rc = heads * hd
    tile = pl.BlockSpec((CHUNK, rc), lambda i, c: (row_block0 + i * n_chunks + c, 0))
    kern = functools.partial(_wkv_chunk_kernel, heads=heads, hd=hd, n_pad=n_pad)
    return pl.pallas_call(
        kern,
        out_shape=(jax.ShapeDtypeStruct((n_seq * n_chunks * CHUNK, rc), F32),
                   jax.ShapeDtypeStruct((n_seq, heads, hd, hd), F32)),
        grid=(n_seq, n_chunks),
        in_specs=[tile] * 6 + [pl.BlockSpec((heads, hd, hd), lambda i, c: (0, 0, 0))],
        out_specs=(pl.BlockSpec((CHUNK, rc), lambda i, c: (i * n_chunks + c, 0)),
                   pl.BlockSpec((1, heads, hd, hd), lambda i, c: (i, 0, 0, 0))),
        scratch_shapes=[pltpu.VMEM((heads, hd, hd), F32)],
        compiler_params=_cparams("parallel", "arbitrary"),
        name="wkv_chunks",
    )(r, lw, k, v, kk, b, s0)


def _wkv_step_kernel(r_ref, lw_ref, k_ref, v_ref, kk_ref, b_ref, s_ref, *rest, hd, slab):
    y_ref, sout_ref = rest[-2:]
    w = jnp.exp(lw_ref[...])
    kk, bb, kx, rx, vx = kk_ref[...], b_ref[...], k_ref[...], r_ref[...], v_ref[...]
    per_slab = slab // hd
    ys = []
    for j in range(hd // per_slab):
        cols = slice(j * slab, (j + 1) * slab)
        st = s_ref[0, :, cols].T
        outs = []
        for t in range(per_slab):
            vi = j * per_slab + t
            s = st[t * hd:(t + 1) * hd, :]
            s_kk = jnp.sum(s * kk, axis=0, keepdims=True)
            s_new = s * w - s_kk * bb + vx[vi:vi + 1, :] * kx
            ys.append(jnp.sum(s_new * rx, axis=0, keepdims=True))
            outs.append(s_new)
        sout_ref[0, :, cols] = jnp.concatenate(outs, axis=0).T
    y_ref[...] = jnp.concatenate(ys, axis=0)


def _wkv_step(r, lw, k, v, kk, b, states, stacked_out, layer, *, heads, hd):
    rc, n_seq = r.shape
    slab = max(hd, 128)
    vec = pl.BlockSpec((hd, n_seq), lambda h: (h, 0))
    st_spec = pl.BlockSpec((1, n_seq, hd * hd), lambda h: (layer, 0, h))
    kern = functools.partial(_wkv_step_kernel, hd=hd, slab=slab)
    extra, extra_specs, aliases = _stacked_alias(stacked_out, 7)
    return pl.pallas_call(
        kern,
        out_shape=(jax.ShapeDtypeStruct((rc, n_seq), F32), jax.ShapeDtypeStruct(states.shape, F32)),
        grid=(heads,),
        in_specs=[vec] * 6 + [st_spec] + extra_specs,
        out_specs=(vec, st_spec),
        input_output_aliases=aliases,
        compiler_params=_cparams("parallel"),
        name="wkv_step",
    )(r, lw, k, v, kk, b, states, *extra)


def _mix_kernel(og_ref, y_ref, r_ref, k_ref, v_ref, g_ref, gates_ref, x_ref,
                wret_ref, wrw_ref, wout_ref, lnw_ref, lnb_ref, rk_ref, gain_ref, o_ref, *, hd, d):
    y_ret = _dot(og_ref[...], wret_ref[...])
    ones = _group_ones(hd)
    y = y_ref[...]
    inv_n = 1.0 / hd
    mean = _group_sum(y, ones) * inv_n
    dlt = y - mean
    var = _group_sum(dlt * dlt, ones) * inv_n
    yn = dlt * lax.rsqrt(var + LNX_EPS) * lnw_ref[...] + lnb_ref[...]
    rkk = r_ref[...].astype(F32) * k_ref[...].astype(F32) * rk_ref[...]
    bonus = _group_sum(rkk, ones) * v_ref[...].astype(F32)
    y_rw = _dot(_bf((yn + bonus) * g_ref[...].astype(F32)), wrw_ref[...])
    gates = gates_ref[...].astype(F32)
    mix = _sigmoid(gates[:, :d]) * y_ret + _sigmoid(gates[:, d:]) * y_rw
    o_ref[...] = x_ref[...] + _rms(_dot(_bf(mix), wout_ref[...]), gain_ref[...])


def _mix(og, y, r, k, v, g, proj, x, w_ret, w_rw, w_out, lnw, lnb, rk, gain, *, tm, hd, gate_block):
    rows, d = x.shape
    tile = lambda a: pl.BlockSpec((tm, a.shape[1]), lambda i: (i, 0))
    const = lambda a: pl.BlockSpec(a.shape, lambda i: (0,) * a.ndim)
    kern = functools.partial(_mix_kernel, hd=hd, d=d)
    return pl.pallas_call(
        kern,
        out_shape=jax.ShapeDtypeStruct((rows, d), F32),
        grid=(rows // tm,),
        in_specs=[tile(og), tile(y), tile(r), tile(k), tile(v), tile(g),
                  pl.BlockSpec((tm, 2 * d), lambda i: (i, gate_block)), tile(x),
                  const(w_ret), const(w_rw), const(w_out), const(lnw), const(lnb), const(rk), const(gain)],
        out_specs=pl.BlockSpec((tm, d), lambda i: (i, 0)),
        compiler_params=_cparams("parallel"),
        name="mix",
    )(og, y, r, k, v, g, proj, x, w_ret, w_rw, w_out, lnw, lnb, rk, gain)


def _ffn_kernel(x_ref, g2_ref, g3_ref, w1_ref, w3_ref, w2_ref, o_ref, hn_ref, acc_ref):
    j = pl.program_id(1)

    @pl.when(j == 0)
    def _():
        hn_ref[...] = _bf(_rms(x_ref[...], g2_ref[...]))
        acc_ref[...] = jnp.zeros_like(acc_ref)

    hn = hn_ref[...]
    a = _dot(hn, w1_ref[...])
    h = (a * _sigmoid(a)) * _dot(hn, w3_ref[...])
    acc_ref[...] += _dot(_bf(h), w2_ref[...])

    @pl.when(j == pl.num_programs(1) - 1)
    def _():
        o_ref[...] = x_ref[...] + _rms(acc_ref[...], g3_ref[...])


def _ffn(x, gain2, gain3, w1, w3, w2, *, tm, tf):
    rows, d = x.shape
    f = w1.shape[1]
    return pl.pallas_call(
        _ffn_kernel,
        out_shape=jax.ShapeDtypeStruct((rows, d), F32),
        grid=(rows // tm, f // tf),
        in_specs=[pl.BlockSpec((tm, d), lambda i, j: (i, 0)),
                  pl.BlockSpec((1, d), lambda i, j: (0, 0)),
                  pl.BlockSpec((1, d), lambda i, j: (0, 0)),
                  pl.BlockSpec((d, tf), lambda i, j: (0, j)),
                  pl.BlockSpec((d, tf), lambda i, j: (0, j)),
                  pl.BlockSpec((tf, d), lambda i, j: (j, 0))],
        out_specs=pl.BlockSpec((tm, d), lambda i, j: (i, 0)),
        scratch_shapes=[pltpu.VMEM((tm, d), BF16), pltpu.VMEM((tm, d), F32)],
        compiler_params=_cparams("parallel", "arbitrary"),
        name="ffn",
    )(x, gain2, gain3, w1, w3, w2)


def _pick(n, prefs):
    for t in prefs:
        if n % t == 0:
            return t
    return n


def _rope_tables(pos, half):
    inv_freq = 1.0 / (ROPE_BASE ** (jnp.arange(half, dtype=F32) / half))
    ang = pos.astype(F32)[:, None] * inv_freq[None, :]
    return jnp.cos(ang), jnp.sin(ang)


def kernel(x_prompt, x_sample, state_ret, state_wkv, state_shift, meta_tokens, norm_gain, w_in, w_ret_out,
           w_rwkv_out, w_out, rwkv_mu, rwkv_w0, rwkv_w2, rwkv_a0, rwkv_a2, rwkv_g2, rwkv_kk, rwkv_ka, rwkv_rk,
           rwkv_lnx_w, rwkv_lnx_b, rwkv_v0, rwkv_v1, rwkv_v2, ffn_w1, ffn_w3, ffn_w2):
    bp, sp, d = x_prompt.shape
    ns = x_sample.shape[0]
    depth = w_in.shape[0]
    n_meta = meta_tokens.shape[0]
    _, _, rh, dk, dv = state_ret.shape
    _, _, wh, hd, _ = state_wkv.shape
    qk, vw, rc = rh * dk, rh * dv, wh * hd
    rw_w = state_shift.shape[-1]
    assert x_sample.shape[1] == 1 and n_meta <= CHUNK and sp % CHUNK == 0
    assert vw == 2 * qk and d == qk and rc == d and hd & (hd - 1) == 0 and MXU_TILE % hd == 0
    n_chunks = sp // CHUNK
    n_pad = CHUNK - n_meta
    main_rows = bp * sp
    small_rows = CHUNK + ns
    half = dk // 2

    xm = x_prompt.reshape(main_rows, d)
    xs = jnp.concatenate([jnp.zeros((n_pad, d), F32), meta_tokens.astype(F32), x_sample.reshape(ns, d)], axis=0)

    cos_m, sin_m = _rope_tables(n_meta + jnp.arange(sp), half)
    cos_t, sin_t = _rope_tables(jnp.arange(CHUNK) - n_pad, half)
    cos_s, sin_s = _rope_tables(jnp.full((1,), PAST_LEN), half)

    split = 2 * qk + 2 * vw
    wa = _bf(jnp.concatenate([w_in[:, :, :split], w_in[:, :, split + rw_w:]], axis=-1))
    wb = _bf(w_in[:, :, split:split + rw_w])
    gate_block = split // (2 * d)
    assert gate_block * 2 * d == split

    tm_m = _pick(main_rows, (512, 256, 128, 64))
    tm_mix = _pick(main_rows, (256, 128, 64))
    tn_a = _pick(wa.shape[-1], (2048, 1024, 512, 256))
    tf = _pick(ffn_w1.shape[-1], (1408, 1024, 768, 512, 384, 256, 128))
    seq_tm = _pick(sp, PREP_TILES)

    zero_row = jnp.zeros((1, rw_w), F32)
    ret_p, wkv_p, sh_p, sh_s = [], [], [], []
    ret_s = wkv_s = None
    wkv_flat = state_wkv.reshape(depth, ns, wh * hd * hd)
    vf_m = vf_s = None
    for l in range(depth):
        g = norm_gain[l]
        p = dict(mu=rwkv_mu[l][None], w0=rwkv_w0[l][None], w2=_bf(rwkv_w2[l]), a0=rwkv_a0[l][None],
                 a2=_bf(rwkv_a2[l]), g2=_bf(rwkv_g2[l]), kk=rwkv_kk[l][None], ka=rwkv_ka[l][None])
        lv = max(l - 1, 0)
        p.update(v0=rwkv_v0[lv][None], v1=_bf(rwkv_v1[lv]), v2=_bf(rwkv_v2[lv]))

        pa_s = _norm_proj(xs, g[0][None], wa[l], small_rows, tn_a)
        pb_s = _norm_proj(xs, g[0][None], wb[l], small_rows, rw_w)
        pb_sf = pb_s.astype(F32)
        prev_s = jnp.concatenate([jnp.zeros((n_pad + 1, rw_w), F32), pb_sf[n_pad:CHUNK - 1], state_shift[l]], axis=0)
        if l == 0:
            vf_s = jnp.zeros((small_rows, rc), BF16)
        r_s, lw_s, k_s, v_s, kk_s, b_s, g_s = _rwkv_prep(pb_s, prev_s, None, p, vf_s, layer0=(l == 0),
                                                         tm=small_rows, seq_rows=small_rows, hd=hd)
        if l == 0:
            vf_s = v_s

        og_meta, s_ret_meta = _ret_chunks(pa_s, cos_t, sin_t, jnp.zeros((rh, dk, dv), F32), row_block0=0,
                                          n_seq=1, n_chunks=1, heads=rh, dk=dk, dv=dv, n_pad=n_pad)
        y_meta, s_wkv_meta = _wkv_chunks(r_s, lw_s, k_s, v_s, kk_s, b_s, jnp.zeros((wh, hd, hd), F32),
                                         row_block0=0, n_seq=1, n_chunks=1, heads=wh, hd=hd, n_pad=n_pad)
        pa_sf = pa_s[CHUNK:].astype(F32)
        og_smp, ret_s = _ret_step(pa_sf[:, :qk], pa_sf[:, qk:2 * qk], pa_sf[:, 2 * qk:2 * qk + vw],
                                  pa_sf[:, 2 * qk + vw:split], cos_s, sin_s, state_ret, ret_s, l,
                                  heads=rh, dk=dk, dv=dv, nb=_pick(ns, (2, 1)))
        smp = [a[CHUNK:].astype(F32).T for a in (r_s, lw_s, k_s, v_s, kk_s, b_s)]
        y_smp, wkv_s = _wkv_step(*smp, wkv_flat, wkv_s, l, heads=wh, hd=hd)
        og_s = jnp.concatenate([og_meta, _bf(og_smp)], axis=0)
        y_s = jnp.concatenate([y_meta, y_smp.T], axis=0)
        sh_s.append(pb_sf[CHUNK:])

        pa_m = _norm_proj(xm, g[0][None], wa[l], tm_m, tn_a)
        pb_m = _norm_proj(xm, g[0][None], wb[l], tm_m, rw_w)
        if l == 0:
            vf_m = jnp.zeros((main_rows, rc), BF16)
        r_m, lw_m, k_m, v_m, kk_m, b_m, g_m = _rwkv_prep(pb_m, None, pb_sf[CHUNK - 1:CHUNK], p, vf_m,
                                                         layer0=(l == 0), tm=seq_tm, seq_rows=sp, hd=hd)
        if l == 0:
            vf_m = v_m
        og_m, s_ret_m = _ret_chunks(pa_m, cos_m, sin_m, s_ret_meta[0], row_block0=0, n_seq=bp,
                                    n_chunks=n_chunks, heads=rh, dk=dk, dv=dv, n_pad=0)
        y_m, s_wkv_m = _wkv_chunks(r_m, lw_m, k_m, v_m, kk_m, b_m, s_wkv_meta[0], row_block0=0, n_seq=bp,
                                   n_chunks=n_chunks, heads=wh, hd=hd, n_pad=0)
        ret_p.append(s_ret_m)
        wkv_p.append(s_wkv_m)
        sh_p.append(pb_m.reshape(bp, sp, rw_w)[:, -1].astype(F32))

        mixw = (_bf(w_ret_out[l]), _bf(w_rwkv_out[l]), _bf(w_out[l]), rwkv_lnx_w[l][None], rwkv_lnx_b[l][None],
                rwkv_rk[l][None], g[1][None])
        ffnw = (g[2][None], g[3][None], _bf(ffn_w1[l]), _bf(ffn_w3[l]), _bf(ffn_w2[l]))
        xs = _mix(og_s, y_s, r_s, k_s, v_s, g_s, pa_s, xs, *mixw, tm=small_rows, hd=hd, gate_block=gate_block)
        xs = _ffn(xs, *ffnw, tm=small_rows, tf=tf)
        xm = _mix(og_m, y_m, r_m, k_m, v_m, g_m, pa_m, xm, *mixw, tm=tm_mix, hd=hd, gate_block=gate_block)
        xm = _ffn(xm, *ffnw, tm=tm_m, tf=tf)

    return (xm.reshape(bp, sp, d), xs[CHUNK:].reshape(ns, 1, d), jnp.stack(ret_p), jnp.stack(wkv_p),
            jnp.stack(sh_p), ret_s, wkv_s.reshape(state_wkv.shape), jnp.stack(sh_s))
```

```python
import functools
import math

import jax
import jax.numpy as jnp
from jax import lax
from jax.experimental import pallas as pl
from jax.experimental.pallas import tpu as pltpu

F32 = jnp.float32
BF16 = jnp.bfloat16

NORM_EPS = 1e-6
LNX_EPS = 64e-5
ROPE_BASE = 10000.0
PAST_LEN = 16384
KK_EPS = 1e-12

CHUNK = 64
MXU_TILE = 256
VMEM_LIMIT = 56 * 1024 * 1024
PREP_TILES = (512, 256, 128, 64)
FFN_SUB_ROWS = 256


def _cparams(*sem):
    return pltpu.CompilerParams(dimension_semantics=sem, vmem_limit_bytes=VMEM_LIMIT)


def _dot(a, b):
    return jnp.dot(a, b, preferred_element_type=F32)


def _dot_nt(a, b):
    return lax.dot_general(a, b, (((1,), (1,)), ((), ())), preferred_element_type=F32)


def _dot_tn(a, b):
    return lax.dot_general(a, b, (((0,), (0,)), ((), ())), preferred_element_type=F32)


def _bf(x):
    return x.astype(BF16)


def _sigmoid(x):
    return 1.0 / (1.0 + jnp.exp(-x))


def _rms(x, gain):
    return x * lax.rsqrt(jnp.mean(x * x, axis=-1, keepdims=True) + NORM_EPS) * gain


def _group_ones(group):
    shift = group.bit_length() - 1
    r = lax.broadcasted_iota(jnp.int32, (MXU_TILE, MXU_TILE), 0) >> shift
    c = lax.broadcasted_iota(jnp.int32, (MXU_TILE, MXU_TILE), 1) >> shift
    return jnp.where(r == c, 1.0, 0.0).astype(BF16)


def _group_sum(x, ones, split=True):
    hi = _bf(x)
    lo = _bf(x - hi.astype(F32)) if split else None
    parts = []
    for j in range(x.shape[1] // MXU_TILE):
        sl = slice(j * MXU_TILE, (j + 1) * MXU_TILE)
        part = _dot(hi[:, sl], ones)
        parts.append(part + _dot(lo[:, sl], ones) if split else part)
    return parts[0] if len(parts) == 1 else jnp.concatenate(parts, axis=1)


def _norm_proj_kernel(x_ref, g_ref, w_ref, o_ref, xn_ref):
    @pl.when(pl.program_id(1) == 0)
    def _():
        xn_ref[...] = _bf(_rms(x_ref[...], g_ref[...]))

    o_ref[...] = _dot(xn_ref[...], w_ref[...]).astype(o_ref.dtype)


def _norm_proj(x, gain, w, layer, tm, tn):
    rows, d = x.shape
    n = w.shape[2]
    return pl.pallas_call(
        _norm_proj_kernel,
        out_shape=jax.ShapeDtypeStruct((rows, n), BF16),
        grid=(rows // tm, n // tn),
        in_specs=[pl.BlockSpec((tm, d), lambda i, j: (i, 0)),
                  pl.BlockSpec((1, d), lambda i, j: (0, 0)),
                  pl.BlockSpec((None, d, tn), lambda i, j: (layer, 0, j))],
        out_specs=pl.BlockSpec((tm, tn), lambda i, j: (i, j)),
        scratch_shapes=[pltpu.VMEM((tm, d), BF16)],
        compiler_params=_cparams("parallel", "arbitrary"),
        name="norm_proj",
    )(x, gain, w)


def _ret_chunk_kernel(q_ref, k_ref, v_ref, gr_ref, cos_ref, sin_ref, s0_ref, og_ref, sout_ref, s_scr,
                      *, heads, dk, dv, n_pad):
    c = pl.program_id(1)
    n_chunks = pl.num_programs(1)
    half = dk // 2

    @pl.when(c == 0)
    def _():
        s_scr[...] = s0_ref[...]

    chunk = q_ref.shape[0]
    cos = cos_ref[...]
    sin = sin_ref[...]
    row = lax.broadcasted_iota(jnp.int32, (chunk, half), 0)
    rowf = row.astype(F32)
    rel_i = lax.broadcasted_iota(jnp.int32, (chunk, chunk), 0)
    rel_j = lax.broadcasted_iota(jnp.int32, (chunk, chunk), 1)
    rel = (rel_i - rel_j).astype(F32)

    def rope(x):
        x1 = x[:, :half]
        x2 = x[:, half:]
        return x1 * cos - x2 * sin, x2 * cos + x1 * sin

    hr = range(heads)
    lgs = [math.log(1.0 - 2.0 ** (-5.0 - h)) for h in hr]
    vsl = [slice(h * dv, (h + 1) * dv) for h in hr]
    qb, kb, qd, kd, vhs, dmask = [], [], [], [], [], []
    for h in hr:
        lg = lgs[h]
        qs = slice(h * dk, (h + 1) * dk)
        q1, q2 = rope(q_ref[:, qs].astype(F32))
        k1, k2 = rope(k_ref[:, qs].astype(F32))
        k1 = k1 * (dk ** -0.5)
        k2 = k2 * (dk ** -0.5)
        vh = v_ref[:, vsl[h]]
        if n_pad:
            k1 = jnp.where(row >= n_pad, k1, 0.0)
            k2 = jnp.where(row >= n_pad, k2, 0.0)
            vrow = lax.broadcasted_iota(jnp.int32, (chunk, dv), 0)
            vh = jnp.where(vrow >= n_pad, vh, jnp.zeros_like(vh))
        q_decay = jnp.exp((rowf + 1.0) * lg)
        k_decay = jnp.exp((chunk - 1.0 - rowf) * lg)
        dmask.append(jnp.where(rel >= 0, jnp.exp(lg * jnp.maximum(rel, 0.0)), 0.0))
        qb.append(_bf(jnp.concatenate([q1, q2], axis=1)))
        kb.append(_bf(jnp.concatenate([k1, k2], axis=1)))
        qd.append(_bf(jnp.concatenate([q1 * q_decay, q2 * q_decay], axis=1)))
        kd.append(_bf(jnp.concatenate([k1 * k_decay, k2 * k_decay], axis=1)))
        vhs.append(vh)
    s_old = [s_scr[h] for h in hr]
    scores = [_bf(_dot_nt(qb[h], kb[h]) * dmask[h]) for h in hr]
    cross = [_dot(qd[h], _bf(s_old[h])) for h in hr]
    inner = [_dot(scores[h], vhs[h]) for h in hr]
    for h in hr:
        o = inner[h] + cross[h]
        o = o * lax.rsqrt(jnp.mean(o * o, axis=-1, keepdims=True) + NORM_EPS)
        g = gr_ref[:, vsl[h]].astype(F32)
        og_ref[:, vsl[h]] = _bf(o * (g * _sigmoid(g)))
    for h in hr:
        s_scr[h] = math.exp(chunk * lgs[h]) * s_old[h] + _dot_tn(kd[h], vhs[h])

    @pl.when(c == n_chunks - 1)
    def _():
        sout_ref[0] = s_scr[...]


def _ret_chunks(proj, cos, sin, s0, *, chunk, n_seq, n_chunks, heads, dk, dv, n_pad):
    qk = heads * dk
    vw = heads * dv
    assert vw == 2 * qk

    def rowmap(col):
        return lambda b, c: (b * n_chunks + c, col)

    kern = functools.partial(_ret_chunk_kernel, heads=heads, dk=dk, dv=dv, n_pad=n_pad)
    return pl.pallas_call(
        kern,
        out_shape=(jax.ShapeDtypeStruct((n_seq * n_chunks * chunk, vw), BF16),
                   jax.ShapeDtypeStruct((n_seq, heads, dk, dv), F32)),
        grid=(n_seq, n_chunks),
        in_specs=[pl.BlockSpec((chunk, qk), rowmap(0)),
                  pl.BlockSpec((chunk, qk), rowmap(1)),
                  pl.BlockSpec((chunk, vw), rowmap(1)),
                  pl.BlockSpec((chunk, vw), rowmap(2)),
                  pl.BlockSpec((chunk, dk // 2), lambda b, c: (c, 0)),
                  pl.BlockSpec((chunk, dk // 2), lambda b, c: (c, 0)),
                  pl.BlockSpec((heads, dk, dv), lambda b, c: (0, 0, 0))],
        out_specs=(pl.BlockSpec((chunk, vw), lambda b, c: (b * n_chunks + c, 0)),
                   pl.BlockSpec((1, heads, dk, dv), lambda b, c: (b, 0, 0, 0))),
        scratch_shapes=[pltpu.VMEM((heads, dk, dv), F32)],
        compiler_params=_cparams("parallel", "arbitrary"),
        name="ret_chunks",
    )(proj, proj, proj, proj, cos, sin, s0)


def _ret_step_kernel(q_ref, k_ref, v_ref, gr_ref, cos_ref, sin_ref, s_ref, *rest, heads, dk, dv, nb):
    og_ref, sout_ref = rest[-2:]
    half = dk // 2
    cos = cos_ref[...]
    sin = sin_ref[...]
    base = pl.program_id(0) * nb
    pad_rows = 16
    lane_w = 128
    sel_r = lax.broadcasted_iota(jnp.int32, (pad_rows, 2 * lane_w), 0)
    sel_c = lax.broadcasted_iota(jnp.int32, (pad_rows, 2 * lane_w), 1)
    spread = jnp.where((sel_r == 0) & (sel_c < lane_w) | (sel_r == 1) & (sel_c >= lane_w), 1.0, 0.0).astype(BF16)
    row_id = lax.broadcasted_iota(jnp.int32, (pad_rows, 1), 0)

    def rope(x):
        x1 = x[:, :half]
        x2 = x[:, half:]
        return jnp.concatenate([x1 * cos - x2 * sin, x2 * cos + x1 * sin], axis=1)

    for i in range(nb):
        n = base + i
        q_row, k_row, v_row, g_row = (ref[pl.ds(n, 1), :] for ref in (q_ref, k_ref, v_ref, gr_ref))
        o_parts = []
        for h in range(heads):
            gamma = 1.0 - 2.0 ** (-5.0 - h)
            qs = slice(h * dk, (h + 1) * dk)
            vs = slice(h * dv, (h + 1) * dv)
            q = rope(q_row[:, qs])
            k = rope(k_row[:, qs]) * (dk ** -0.5)
            v = v_row[:, vs]
            kq_rows = jnp.where(row_id == 0, jnp.broadcast_to(k, (pad_rows, dk)),
                                jnp.where(row_id == 1, jnp.broadcast_to(q, (pad_rows, dk)), 0.0))
            cols = _dot_tn(_bf(kq_rows), spread)
            k_col, q_col = cols[:, :lane_w], cols[:, lane_w:]
            o_tiles = []
            for t in range(dv // lane_w):
                ts = slice(t * lane_w, (t + 1) * lane_w)
                s_new = gamma * s_ref[0, i, h, :, ts] + k_col * v[:, ts]
                sout_ref[0, i, h, :, ts] = s_new
                o_tiles.append(jnp.sum(q_col * s_new, axis=0, keepdims=True))
            o = jnp.concatenate(o_tiles, axis=1)
            o = o * lax.rsqrt(jnp.mean(o * o, axis=-1, keepdims=True) + NORM_EPS)
            g = g_row[:, vs]
            o_parts.append(o * (g * _sigmoid(g)))
        og_ref[pl.ds(n, 1), :] = jnp.concatenate(o_parts, axis=1)


def _stacked_alias(stacked_out, n_inputs):
    if stacked_out is None:
        return [], [], {}
    return [stacked_out], [pl.BlockSpec(memory_space=pl.ANY)], {n_inputs: 1}


def _ret_step(q, k, v, gr, cos, sin, states, stacked_out, layer, *, heads, dk, dv, nb):
    n_seq = q.shape[0]
    full = lambda a: pl.BlockSpec(a.shape, lambda i: (0,) * a.ndim)
    kern = functools.partial(_ret_step_kernel, heads=heads, dk=dk, dv=dv, nb=nb)
    st_spec = pl.BlockSpec((1, nb, heads, dk, dv), lambda i: (layer, i, 0, 0, 0))
    extra, extra_specs, aliases = _stacked_alias(stacked_out, 7)
    return pl.pallas_call(
        kern,
        out_shape=(jax.ShapeDtypeStruct((n_seq, heads * dv), F32),
                   jax.ShapeDtypeStruct(states.shape, F32)),
        grid=(n_seq // nb,),
        in_specs=[full(q), full(k), full(v), full(gr), full(cos), full(sin), st_spec] + extra_specs,
        out_specs=(pl.BlockSpec((n_seq, heads * dv), lambda i: (0, 0)), st_spec),
        input_output_aliases=aliases,
        compiler_params=_cparams("arbitrary"),
        name="ret_step",
    )(q, k, v, gr, cos, sin, states, *extra)


def _rwkv_prep_kernel(*refs, layer0, full_prev, tm, rc, hd, lw_, la_, seq_tiles):
    if full_prev:
        (rw_ref, prev_ref, mu_ref, w0_ref, w2_ref, a0_ref, a2_ref, g2_ref, kkp_ref, kap_ref,
         v0_ref, v1_ref, v2_ref, vf_ref, r_o, lw_o, k_o, v_o, kk_o, b_o, g_o) = refs
    else:
        (rw_ref, tail_ref, start_ref, mu_ref, w0_ref, w2_ref, a0_ref, a2_ref, g2_ref, kkp_ref, kap_ref,
         v0_ref, v1_ref, v2_ref, vf_ref, r_o, lw_o, k_o, v_o, kk_o, b_o, g_o) = refs
        i = pl.program_id(0)
        is_start = (i % seq_tiles) == 0
        row0 = lax.broadcasted_iota(jnp.int32, (tm, 1), 0) == 0

    def mixed(sl):
        cur = rw_ref[:, sl].astype(F32)
        if full_prev:
            prev = prev_ref[:, sl]
        else:
            first = jnp.where(is_start, start_ref[:, sl], tail_ref[15:16, sl].astype(F32))
            prev = jnp.where(row0, first, pltpu.roll(cur, 1, 0))
        return cur + (prev - cur) * mu_ref[:, sl]

    z_l = mixed(slice(3 * rc, rw_ref.shape[1]))
    wd = z_l[:, :lw_]
    ad = z_l[:, lw_:lw_ + la_]
    gd = z_l[:, lw_ + la_:]
    w_in = w0_ref[...] + _dot(_bf(jnp.tanh(wd)), w2_ref[...])
    neg = -w_in
    softplus = jnp.maximum(neg, 0.0) + jnp.log(1.0 + jnp.exp(-jnp.abs(neg)))
    lw_o[...] = -jnp.exp(-softplus - 0.5)
    a = _sigmoid(a0_ref[...] + _dot(_bf(ad), a2_ref[...]))
    g_o[...] = _bf(_dot(_bf(_sigmoid(gd)), g2_ref[...]))

    r_o[...] = _bf(mixed(slice(0, rc)))

    z_k = mixed(slice(rc, 2 * rc))
    kk = z_k * kkp_ref[...]
    ones = _group_ones(hd)
    norm = jnp.maximum(jnp.sqrt(_group_sum(kk * kk, ones)), KK_EPS)
    kk = kk / norm
    kk_o[...] = _bf(kk)
    b_o[...] = _bf(kk * a)
    k_o[...] = _bf(z_k * (1.0 + (a - 1.0) * kap_ref[...]))

    z_v = mixed(slice(2 * rc, 3 * rc))
    if layer0:
        v_o[...] = _bf(z_v)
    else:
        lora = _dot(_bf(_dot(_bf(z_v), v1_ref[...])), v2_ref[...])
        v_o[...] = _bf(z_v + (vf_ref[...].astype(F32) - z_v) * _sigmoid(v0_ref[...] + lora))


def _rwkv_prep(rw, prev, start_row, p, v_first, *, layer0, tm, seq_rows, hd):
    rows, width = rw.shape
    rc = p["w0"].shape[1]
    lw_ = p["w2"].shape[0]
    la_ = p["a2"].shape[0]
    full_prev = prev is not None
    tile = lambda w: pl.BlockSpec((tm, w), lambda i: (i, 0))
    const = lambda a: pl.BlockSpec(a.shape, lambda i: (0,) * a.ndim)
    params = [p["mu"], p["w0"], p["w2"], p["a0"], p["a2"], p["g2"], p["kk"], p["ka"], p["v0"], p["v1"], p["v2"]]
    if full_prev:
        lead, lead_specs = [rw, prev], [tile(width), tile(width)]
    else:
        assert tm % 16 == 0 and seq_rows % tm == 0
        tail_spec = pl.BlockSpec((16, width), lambda i: (jnp.maximum(i * (tm // 16) - 1, 0), 0))
        lead, lead_specs = [rw, rw, start_row], [tile(width), tail_spec, const(start_row)]
    kern = functools.partial(_rwkv_prep_kernel, layer0=layer0, full_prev=full_prev, tm=tm, rc=rc, hd=hd,
                             lw_=lw_, la_=la_, seq_tiles=max(seq_rows // tm, 1))
    out_bf = jax.ShapeDtypeStruct((rows, rc), BF16)
    return pl.pallas_call(
        kern,
        out_shape=(out_bf, jax.ShapeDtypeStruct((rows, rc), F32), out_bf, out_bf, out_bf, out_bf, out_bf),
        grid=(rows // tm,),
        in_specs=lead_specs + [const(a) for a in params] + [tile(rc)],
        out_specs=tuple(tile(rc) for _ in range(7)),
        compiler_params=_cparams("parallel"),
        name="rwkv_prep",
    )(*lead, *params, v_first)


def _wkv_chunk_kernel(r_ref, lw_ref, k_ref, v_ref, kk_ref, b_ref, s0_ref, y_ref, sout_ref, s_scr,
                      *, heads, hd, n_pad):
    c = pl.program_id(1)
    n_chunks = pl.num_programs(1)
    C = CHUNK

    @pl.when(c == 0)
    def _():
        s_scr[...] = s0_ref[...]

    ti = lax.broadcasted_iota(jnp.int32, (C, C), 0)
    tj = lax.broadcasted_iota(jnp.int32, (C, C), 1)
    lower = ti >= tj
    strict = ti > tj
    tril = jnp.where(lower, 1.0, 0.0).astype(BF16)

    lw = lw_ref[...]
    hi = _bf(lw)
    r1 = lw - hi.astype(F32)
    mid = _bf(r1)
    lo = _bf(r1 - mid.astype(F32))
    cum = _dot(tril, hi) + _dot(tril, mid) + _dot(tril, lo)
    e_in = jnp.exp(cum)
    e_inv = jnp.exp(-cum)
    e_prev = jnp.exp(cum - lw)
    total = cum[C - 1:C, :]
    e_tail = jnp.exp(total - cum)
    p_end = jnp.exp(total)

    kk = kk_ref[...].astype(F32)
    bb = b_ref[...].astype(F32)
    kx = k_ref[...].astype(F32)
    vx = v_ref[...]
    if n_pad:
        rowm = lax.broadcasted_iota(jnp.int32, kk.shape, 0) >= n_pad
        kk = jnp.where(rowm, kk, 0.0)
        bb = jnp.where(rowm, bb, 0.0)
        kx = jnp.where(rowm, kx, 0.0)
        vx = jnp.where(rowm, vx, jnp.zeros_like(vx))
    a_t = _bf(-kk * e_prev)
    b_t = _bf(bb * e_inv)
    k_t = _bf(kx * e_inv)
    r_t = _bf(r_ref[...].astype(F32) * e_in)
    b_hat = _bf(bb * e_tail)
    k_hat = _bf(kx * e_tail)

    hr = range(heads)
    hs = [slice(h * hd, (h + 1) * hd) for h in hr]
    ti2 = lax.broadcasted_iota(jnp.int32, (C, 2 * C), 0)
    tj2 = lax.broadcasted_iota(jnp.int32, (C, 2 * C), 1) & (C - 1)
    strict2 = ti2 > tj2
    lower2 = ti2 >= tj2
    s0 = [s_scr[h] for h in hr]
    s0b = [_bf(s) for s in s0]
    s4 = [_dot_nt(jnp.concatenate([a_t[:, hs[h]], r_t[:, hs[h]]], axis=0),
                  jnp.concatenate([b_t[:, hs[h]], k_t[:, hs[h]]], axis=0)) for h in hr]
    top = [jnp.where(strict2, s[:C], 0.0) for s in s4]
    bot = [_bf(jnp.where(lower2, s[C:], 0.0)) for s in s4]
    m = [t[:, :C] for t in top]
    x = [_dot_nt(a_t[:, hs[h]], s0b[h]) + _dot(_bf(top[h][:, C:]), vx[:, hs[h]]) for h in hr]
    for _ in range(5):
        mb = [_bf(v) for v in m]
        wide = [_dot(mb[h], jnp.concatenate([mb[h], _bf(x[h])], axis=1)) for h in hr]
        m = [w[:, :C] for w in wide]
        x = [x[h] + wide[h][:, C:] for h in hr]
    u = [x[h] + _dot(_bf(m[h]), _bf(x[h])) for h in hr]
    uv = [jnp.concatenate([_bf(u[h]), vx[:, hs[h]]], axis=0) for h in hr]
    for h in hr:
        y_ref[:, hs[h]] = _dot_nt(r_t[:, hs[h]], s0b[h]) + _dot(bot[h], uv[h])
    for h in hr:
        s_scr[h] = s0[h] * p_end[:, hs[h]] + _dot_tn(
            uv[h], jnp.concatenate([b_hat[:, hs[h]], k_hat[:, hs[h]]], axis=0))

    @pl.when(c == n_chunks - 1)
    def _():
        sout_ref[0] = s_scr[...]


def _wkv_chunks(r, lw, k, v, kk, b, s0, *, row_block0, n_seq, n_chunks, heads, hd, n_pad):
    rc = heads * hd
    tile = pl.BlockSpec((CHUNK, rc), lambda i, c: (row_block0 + i * n_chunks + c, 0))
    kern = functools.partial(_wkv_chunk_kernel, heads=heads, hd=hd, n_pad=n_pad)
    return pl.pallas_call(
        kern,
        out_shape=(jax.ShapeDtypeStruct((n_seq * n_chunks * CHUNK, rc), F32),
                   jax.ShapeDtypeStruct((n_seq, heads, hd, hd), F32)),
        grid=(n_seq, n_chunks),
        in_specs=[tile] * 6 + [pl.BlockSpec((heads, hd, hd), lambda i, c: (0, 0, 0))],
        out_specs=(pl.BlockSpec((CHUNK, rc), lambda i, c: (i * n_chunks + c, 0)),
                   pl.BlockSpec((1, heads, hd, hd), lambda i, c: (i, 0, 0, 0))),
        scratch_shapes=[pltpu.VMEM((heads, hd, hd), F32)],
        compiler_params=_cparams("parallel", "arbitrary"),
        name="wkv_chunks",
    )(r, lw, k, v, kk, b, s0)


def _wkv_step_kernel(r_ref, lw_ref, k_ref, v_ref, kk_ref, b_ref, s_ref, *rest, hd, slab):
    y_ref, sout_ref = rest[-2:]
    w = jnp.exp(lw_ref[...])
    kk, bb, kx, rx, vx = kk_ref[...], b_ref[...], k_ref[...], r_ref[...], v_ref[...]
    per_slab = slab // hd
    ys = []
    for j in range(hd // per_slab):
        cols = slice(j * slab, (j + 1) * slab)
        st = s_ref[0, :, cols].T
        outs = []
        for t in range(per_slab):
            vi = j * per_slab + t
            s = st[t * hd:(t + 1) * hd, :]
            s_kk = jnp.sum(s * kk, axis=0, keepdims=True)
            s_new = s * w - s_kk * bb + vx[vi:vi + 1, :] * kx
            ys.append(jnp.sum(s_new * rx, axis=0, keepdims=True))
            outs.append(s_new)
        sout_ref[0, :, cols] = jnp.concatenate(outs, axis=0).T
    y_ref[...] = jnp.concatenate(ys, axis=0)


def _wkv_step(r, lw, k, v, kk, b, states, stacked_out, layer, *, heads, hd):
    rc, n_seq = r.shape
    slab = max(hd, 128)
    vec = pl.BlockSpec((hd, n_seq), lambda h: (h, 0))
    st_spec = pl.BlockSpec((1, n_seq, hd * hd), lambda h: (layer, 0, h))
    kern = functools.partial(_wkv_step_kernel, hd=hd, slab=slab)
    extra, extra_specs, aliases = _stacked_alias(stacked_out, 7)
    return pl.pallas_call(
        kern,
        out_shape=(jax.ShapeDtypeStruct((rc, n_seq), F32), jax.ShapeDtypeStruct(states.shape, F32)),
        grid=(heads,),
        in_specs=[vec] * 6 + [st_spec] + extra_specs,
        out_specs=(vec, st_spec),
        input_output_aliases=aliases,
        compiler_params=_cparams("parallel"),
        name="wkv_step",
    )(r, lw, k, v, kk, b, states, *extra)


def _mix_kernel(og_ref, y_ref, r_ref, k_ref, v_ref, g_ref, gates_ref, x_ref,
                wret_ref, wrw_ref, wout_ref, lnw_ref, lnb_ref, rk_ref, gain_ref, o_ref, *, hd, d):
    y_ret = _dot(og_ref[...], wret_ref[...])
    ones = _group_ones(hd)
    y = y_ref[...]
    inv_n = 1.0 / hd
    mean = _group_sum(y, ones, split=False) * inv_n
    dlt = y - mean
    var = _group_sum(dlt * dlt, ones) * inv_n
    yn = dlt * lax.rsqrt(var + LNX_EPS) * lnw_ref[...] + lnb_ref[...]
    rkk = r_ref[...].astype(F32) * k_ref[...].astype(F32) * rk_ref[...]
    bonus = _group_sum(rkk, ones, split=False) * v_ref[...].astype(F32)
    y_rw = _dot(_bf((yn + bonus) * g_ref[...].astype(F32)), wrw_ref[...])
    gates = gates_ref[...].astype(F32)
    mix = _sigmoid(gates[:, :d]) * y_ret + _sigmoid(gates[:, d:]) * y_rw
    o_ref[...] = x_ref[...] + _rms(_dot(_bf(mix), wout_ref[...]), gain_ref[...])


def _mix(og, y, r, k, v, g, proj, x, w_ret, w_rw, w_out, lnw, lnb, rk, gain, *, layer, tm, hd, gate_block):
    rows, d = x.shape
    tile = lambda a: pl.BlockSpec((tm, a.shape[1]), lambda i: (i, 0))
    const = lambda a: pl.BlockSpec(a.shape, lambda i: (0,) * a.ndim)
    stacked = lambda a: pl.BlockSpec((None,) + a.shape[1:], lambda i: (layer, 0, 0),
                                     pipeline_mode=pl.Buffered(1))
    kern = functools.partial(_mix_kernel, hd=hd, d=d)
    return pl.pallas_call(
        kern,
        out_shape=jax.ShapeDtypeStruct((rows, d), F32),
        grid=(rows // tm,),
        in_specs=[tile(og), tile(y), tile(r), tile(k), tile(v), tile(g),
                  pl.BlockSpec((tm, 2 * d), lambda i: (i, gate_block)), tile(x),
                  stacked(w_ret), stacked(w_rw), stacked(w_out), const(lnw), const(lnb), const(rk), const(gain)],
        out_specs=pl.BlockSpec((tm, d), lambda i: (i, 0)),
        compiler_params=_cparams("parallel"),
        name="mix",
    )(og, y, r, k, v, g, proj, x, w_ret, w_rw, w_out, lnw, lnb, rk, gain)


def _ffn_kernel(x_ref, g2_ref, g3_ref, w1_ref, w3_ref, w2_ref, o_ref, hn_ref, acc_ref):
    j = pl.program_id(1)

    @pl.when(j == 0)
    def _():
        hn_ref[...] = _bf(_rms(x_ref[...], g2_ref[...]))
        acc_ref[...] = jnp.zeros_like(acc_ref)

    tm = hn_ref.shape[0]
    sub = FFN_SUB_ROWS if tm % FFN_SUB_ROWS == 0 else tm
    blocks = [slice(r, r + sub) for r in range(0, tm, sub)]

    def up(rows):
        hn = hn_ref[rows, :]
        return _dot(hn, w1_ref[...]), _dot(hn, w3_ref[...])

    def down(rows, ab):
        a, b = ab
        acc_ref[rows, :] += _dot(_bf((a * _sigmoid(a)) * b), w2_ref[...])

    pending = up(blocks[0])
    for r in range(1, len(blocks)):
        nxt = up(blocks[r])
        down(blocks[r - 1], pending)
        pending = nxt
    down(blocks[-1], pending)

    @pl.when(j == pl.num_programs(1) - 1)
    def _():
        o_ref[...] = x_ref[...] + _rms(acc_ref[...], g3_ref[...])


def _ffn(x, gain2, gain3, w1, w3, w2, *, layer, tm, tf):
    rows, d = x.shape
    f = w1.shape[2]
    mode = dict(pipeline_mode=pl.Buffered(1)) if tf == f else {}
    return pl.pallas_call(
        _ffn_kernel,
        out_shape=jax.ShapeDtypeStruct((rows, d), F32),
        grid=(rows // tm, f // tf),
        in_specs=[pl.BlockSpec((tm, d), lambda i, j: (i, 0)),
                  pl.BlockSpec((1, d), lambda i, j: (0, 0)),
                  pl.BlockSpec((1, d), lambda i, j: (0, 0)),
                  pl.BlockSpec((None, d, tf), lambda i, j: (layer, 0, j), **mode),
                  pl.BlockSpec((None, d, tf), lambda i, j: (layer, 0, j), **mode),
                  pl.BlockSpec((None, tf, d), lambda i, j: (layer, j, 0), **mode)],
        out_specs=pl.BlockSpec((tm, d), lambda i, j: (i, 0)),
        scratch_shapes=[pltpu.VMEM((tm, d), BF16), pltpu.VMEM((tm, d), F32)],
        compiler_params=_cparams("parallel", "arbitrary"),
        name="ffn",
    )(x, gain2, gain3, w1, w3, w2)


def _pick(n, prefs):
    for t in prefs:
        if n % t == 0:
            return t
    return n


def _rope_tables(pos, half):
    inv_freq = 1.0 / (ROPE_BASE ** (jnp.arange(half, dtype=F32) / half))
    ang = pos.astype(F32)[:, None] * inv_freq[None, :]
    return jnp.cos(ang), jnp.sin(ang)


def kernel(x_prompt, x_sample, state_ret, state_wkv, state_shift, meta_tokens, norm_gain, w_in, w_ret_out,
           w_rwkv_out, w_out, rwkv_mu, rwkv_w0, rwkv_w2, rwkv_a0, rwkv_a2, rwkv_g2, rwkv_kk, rwkv_ka, rwkv_rk,
           rwkv_lnx_w, rwkv_lnx_b, rwkv_v0, rwkv_v1, rwkv_v2, ffn_w1, ffn_w3, ffn_w2):
    bp, sp, d = x_prompt.shape
    ns = x_sample.shape[0]
    depth = w_in.shape[0]
    n_meta = meta_tokens.shape[0]
    _, _, rh, dk, dv = state_ret.shape
    _, _, wh, hd, _ = state_wkv.shape
    qk, vw, rc = rh * dk, rh * dv, wh * hd
    rw_w = state_shift.shape[-1]
    assert x_sample.shape[1] == 1 and n_meta <= CHUNK and sp % CHUNK == 0
    assert vw == 2 * qk and d == qk and rc == d and hd & (hd - 1) == 0 and MXU_TILE % hd == 0
    n_chunks = sp // CHUNK
    n_pad = CHUNK - n_meta
    main_rows = bp * sp
    small_rows = CHUNK + ns
    half = dk // 2

    xm = x_prompt.reshape(main_rows, d)
    xs = jnp.concatenate([jnp.zeros((n_pad, d), F32), meta_tokens.astype(F32), x_sample.reshape(ns, d)], axis=0)

    cos_m, sin_m = _rope_tables(n_meta + jnp.arange(sp), half)
    cos_t, sin_t = _rope_tables(jnp.arange(CHUNK) - n_pad, half)
    cos_s, sin_s = _rope_tables(jnp.full((1,), PAST_LEN), half)

    split = 2 * qk + 2 * vw
    wa = _bf(jnp.concatenate([w_in[:, :, :split], w_in[:, :, split + rw_w:]], axis=-1))
    wb = _bf(w_in[:, :, split:split + rw_w])
    gate_block = split // (2 * d)
    assert gate_block * 2 * d == split

    tm_m = _pick(main_rows, (1024, 512, 256, 128, 64))
    tm_mix = _pick(main_rows, (512, 256, 128, 64))
    ret_chunk = _pick(sp, (2 * CHUNK, CHUNK))
    tn_a = _pick(wa.shape[-1], (2048, 1024, 512, 256))
    tf = ffn_w1.shape[-1]
    seq_tm = _pick(sp, PREP_TILES)

    w_ret_b, w_rw_b, w_out_b = _bf(w_ret_out), _bf(w_rwkv_out), _bf(w_out)
    w1_b, w3_b, w2_b = _bf(ffn_w1), _bf(ffn_w3), _bf(ffn_w2)
    ret_p, wkv_p, sh_p, sh_s = [], [], [], []
    ret_s = wkv_s = None
    wkv_flat = state_wkv.reshape(depth, ns, wh * hd * hd)
    vf_m = vf_s = None
    for l in range(depth):
        g = norm_gain[l]
        p = dict(mu=rwkv_mu[l][None], w0=rwkv_w0[l][None], w2=_bf(rwkv_w2[l]), a0=rwkv_a0[l][None],
                 a2=_bf(rwkv_a2[l]), g2=_bf(rwkv_g2[l]), kk=rwkv_kk[l][None], ka=rwkv_ka[l][None])
        lv = max(l - 1, 0)
        p.update(v0=rwkv_v0[lv][None], v1=_bf(rwkv_v1[lv]), v2=_bf(rwkv_v2[lv]))

        pa_s = _norm_proj(xs, g[0][None], wa, l, small_rows, tn_a)
        pb_s = _norm_proj(xs, g[0][None], wb, l, small_rows, rw_w)
        pb_sf = pb_s.astype(F32)
        prev_s = jnp.concatenate([jnp.zeros((n_pad + 1, rw_w), F32), pb_sf[n_pad:CHUNK - 1], state_shift[l]], axis=0)
        if l == 0:
            vf_s = jnp.zeros((small_rows, rc), BF16)
        r_s, lw_s, k_s, v_s, kk_s, b_s, g_s = _rwkv_prep(pb_s, prev_s, None, p, vf_s, layer0=(l == 0),
                                                         tm=small_rows, seq_rows=small_rows, hd=hd)
        if l == 0:
            vf_s = v_s

        og_meta, s_ret_meta = _ret_chunks(pa_s, cos_t, sin_t, jnp.zeros((rh, dk, dv), F32), chunk=CHUNK,
                                          n_seq=1, n_chunks=1, heads=rh, dk=dk, dv=dv, n_pad=n_pad)
        y_meta, s_wkv_meta = _wkv_chunks(r_s, lw_s, k_s, v_s, kk_s, b_s, jnp.zeros((wh, hd, hd), F32),
                                         row_block0=0, n_seq=1, n_chunks=1, heads=wh, hd=hd, n_pad=n_pad)
        pa_sf = pa_s[CHUNK:].astype(F32)
        og_smp, ret_s = _ret_step(pa_sf[:, :qk], pa_sf[:, qk:2 * qk], pa_sf[:, 2 * qk:2 * qk + vw],
                                  pa_sf[:, 2 * qk + vw:split], cos_s, sin_s, state_ret, ret_s, l,
                                  heads=rh, dk=dk, dv=dv, nb=_pick(ns, (2, 1)))
        smp = [a[CHUNK:].astype(F32).T for a in (r_s, lw_s, k_s, v_s, kk_s, b_s)]
        y_smp, wkv_s = _wkv_step(*smp, wkv_flat, wkv_s, l, heads=wh, hd=hd)
        og_s = jnp.concatenate([og_meta, _bf(og_smp)], axis=0)
        y_s = jnp.concatenate([y_meta, y_smp.T], axis=0)
        sh_s.append(pb_sf[CHUNK:])

        pa_m = _norm_proj(xm, g[0][None], wa, l, tm_m, tn_a)
        pb_m = _norm_proj(xm, g[0][None], wb, l, tm_m, rw_w)
        if l == 0:
            vf_m = jnp.zeros((main_rows, rc), BF16)
        r_m, lw_m, k_m, v_m, kk_m, b_m, g_m = _rwkv_prep(pb_m, None, pb_sf[CHUNK - 1:CHUNK], p, vf_m,
                                                         layer0=(l == 0), tm=seq_tm, seq_rows=sp, hd=hd)
        if l == 0:
            vf_m = v_m
        og_m, s_ret_m = _ret_chunks(pa_m, cos_m, sin_m, s_ret_meta[0], chunk=ret_chunk, n_seq=bp,
                                    n_chunks=sp // ret_chunk, heads=rh, dk=dk, dv=dv, n_pad=0)
        y_m, s_wkv_m = _wkv_chunks(r_m, lw_m, k_m, v_m, kk_m, b_m, s_wkv_meta[0], row_block0=0, n_seq=bp,
                                   n_chunks=n_chunks, heads=wh, hd=hd, n_pad=0)
        ret_p.append(s_ret_m)
        wkv_p.append(s_wkv_m)
        sh_p.append(pb_m.reshape(bp, sp, rw_w)[:, -1].astype(F32))

        mixw = (w_ret_b, w_rw_b, w_out_b, rwkv_lnx_w[l][None], rwkv_lnx_b[l][None], rwkv_rk[l][None], g[1][None])
        ffnw = (g[2][None], g[3][None], w1_b, w3_b, w2_b)
        mix_kw = dict(layer=l, hd=hd, gate_block=gate_block)
        xs = _mix(og_s, y_s, r_s, k_s, v_s, g_s, pa_s, xs, *mixw, tm=small_rows, **mix_kw)
        xs = _ffn(xs, *ffnw, layer=l, tm=small_rows, tf=tf)
        xm = _mix(og_m, y_m, r_m, k_m, v_m, g_m, pa_m, xm, *mixw, tm=tm_mix, **mix_kw)
        xm = _ffn(xm, *ffnw, layer=l, tm=tm_m, tf=tf)

    return (xm.reshape(bp, sp, d), xs[CHUNK:].reshape(ns, 1, d), jnp.stack(ret_p), jnp.stack(wkv_p),
            jnp.stack(sh_p), ret_s, wkv_s.reshape(state_wkv.shape), jnp.stack(sh_s))
```

```python
import functools
import math

import jax
import jax.numpy as jnp
from jax import lax
from jax.experimental import pallas as pl
from jax.experimental.pallas import tpu as pltpu

F32 = jnp.float32
BF16 = jnp.bfloat16

NORM_EPS = 1e-6
LNX_EPS = 64e-5
ROPE_BASE = 10000.0
PAST_LEN = 16384
KK_EPS = 1e-12

CHUNK = 64
MXU_TILE = 256
VMEM_LIMIT = 56 * 1024 * 1024
PREP_TILES = (512, 256, 128, 64)
FFN_SUB_ROWS = 256


def _cparams(*sem):
    return pltpu.CompilerParams(dimension_semantics=sem, vmem_limit_bytes=VMEM_LIMIT)


def _dot(a, b):
    return jnp.dot(a, b, preferred_element_type=F32)


def _dot_nt(a, b):
    return lax.dot_general(a, b, (((1,), (1,)), ((), ())), preferred_element_type=F32)


def _dot_tn(a, b):
    return lax.dot_general(a, b, (((0,), (0,)), ((), ())), preferred_element_type=F32)


def _bf(x):
    return x.astype(BF16)


def _sigmoid(x):
    return 1.0 / (1.0 + jnp.exp(-x))


def _rms(x, gain):
    return x * lax.rsqrt(jnp.mean(x * x, axis=-1, keepdims=True) + NORM_EPS) * gain


def _group_ones(group):
    shift = group.bit_length() - 1
    r = lax.broadcasted_iota(jnp.int32, (MXU_TILE, MXU_TILE), 0) >> shift
    c = lax.broadcasted_iota(jnp.int32, (MXU_TILE, MXU_TILE), 1) >> shift
    return jnp.where(r == c, 1.0, 0.0).astype(BF16)


def _group_sum(x, ones, split=True):
    hi = _bf(x)
    lo = _bf(x - hi.astype(F32)) if split else None
    parts = []
    for j in range(x.shape[1] // MXU_TILE):
        sl = slice(j * MXU_TILE, (j + 1) * MXU_TILE)
        part = _dot(hi[:, sl], ones)
        parts.append(part + _dot(lo[:, sl], ones) if split else part)
    return parts[0] if len(parts) == 1 else jnp.concatenate(parts, axis=1)


def _norm_proj_kernel(x_ref, g_ref, w_ref, o_ref, xn_ref):
    @pl.when(pl.program_id(1) == 0)
    def _():
        xn_ref[...] = _bf(_rms(x_ref[...], g_ref[...]))

    o_ref[...] = _dot(xn_ref[...], w_ref[...]).astype(o_ref.dtype)


def _norm_proj(x, gain, w, layer, tm, tn):
    rows, d = x.shape
    n = w.shape[2]
    return pl.pallas_call(
        _norm_proj_kernel,
        out_shape=jax.ShapeDtypeStruct((rows, n), BF16),
        grid=(rows // tm, n // tn),
        in_specs=[pl.BlockSpec((tm, d), lambda i, j: (i, 0)),
                  pl.BlockSpec((1, d), lambda i, j: (0, 0)),
                  pl.BlockSpec((None, d, tn), lambda i, j: (layer, 0, j))],
        out_specs=pl.BlockSpec((tm, tn), lambda i, j: (i, j)),
        scratch_shapes=[pltpu.VMEM((tm, d), BF16)],
        compiler_params=_cparams("parallel", "arbitrary"),
        name="norm_proj",
    )(x, gain, w)


def _ret_chunk_kernel(q_ref, k_ref, v_ref, gr_ref, cos_ref, sin_ref, s0_ref, og_ref, sout_ref, s_scr,
                      *, heads, dk, dv, n_pad):
    c = pl.program_id(1)
    n_chunks = pl.num_programs(1)
    half = dk // 2

    @pl.when(c == 0)
    def _():
        s_scr[...] = s0_ref[...]

    chunk = q_ref.shape[0]
    cos = cos_ref[...]
    sin = sin_ref[...]
    row = lax.broadcasted_iota(jnp.int32, (chunk, half), 0)
    rowf = row.astype(F32)
    rel_i = lax.broadcasted_iota(jnp.int32, (chunk, chunk), 0)
    rel_j = lax.broadcasted_iota(jnp.int32, (chunk, chunk), 1)
    rel = (rel_i - rel_j).astype(F32)

    def rope(x):
        x1 = x[:, :half]
        x2 = x[:, half:]
        return x1 * cos - x2 * sin, x2 * cos + x1 * sin

    hr = range(heads)
    lgs = [math.log(1.0 - 2.0 ** (-5.0 - h)) for h in hr]
    vsl = [slice(h * dv, (h + 1) * dv) for h in hr]
    qb, kb, qd, kd, vhs, dmask = [], [], [], [], [], []
    for h in hr:
        lg = lgs[h]
        qs = slice(h * dk, (h + 1) * dk)
        q1, q2 = rope(q_ref[:, qs].astype(F32))
        k1, k2 = rope(k_ref[:, qs].astype(F32))
        k1 = k1 * (dk ** -0.5)
        k2 = k2 * (dk ** -0.5)
        vh = v_ref[:, vsl[h]]
        if n_pad:
            k1 = jnp.where(row >= n_pad, k1, 0.0)
            k2 = jnp.where(row >= n_pad, k2, 0.0)
            vrow = lax.broadcasted_iota(jnp.int32, (chunk, dv), 0)
            vh = jnp.where(vrow >= n_pad, vh, jnp.zeros_like(vh))
        q_decay = jnp.exp((rowf + 1.0) * lg)
        k_decay = jnp.exp((chunk - 1.0 - rowf) * lg)
        dmask.append(jnp.where(rel >= 0, jnp.exp(lg * jnp.maximum(rel, 0.0)), 0.0))
        qb.append(_bf(jnp.concatenate([q1, q2], axis=1)))
        kb.append(_bf(jnp.concatenate([k1, k2], axis=1)))
        qd.append(_bf(jnp.concatenate([q1 * q_decay, q2 * q_decay], axis=1)))
        kd.append(_bf(jnp.concatenate([k1 * k_decay, k2 * k_decay], axis=1)))
        vhs.append(vh)
    s_old = [s_scr[h] for h in hr]
    scores = [_bf(_dot_nt(qb[h], kb[h]) * dmask[h]) for h in hr]
    cross = [_dot(qd[h], _bf(s_old[h])) for h in hr]
    inner = [_dot(scores[h], vhs[h]) for h in hr]
    for h in hr:
        o = inner[h] + cross[h]
        o = o * lax.rsqrt(jnp.mean(o * o, axis=-1, keepdims=True) + NORM_EPS)
        g = gr_ref[:, vsl[h]].astype(F32)
        og_ref[:, vsl[h]] = _bf(o * (g * _sigmoid(g)))
    for h in hr:
        s_scr[h] = math.exp(chunk * lgs[h]) * s_old[h] + _dot_tn(kd[h], vhs[h])

    @pl.when(c == n_chunks - 1)
    def _():
        sout_ref[0] = s_scr[...]


def _ret_chunks(proj, cos, sin, s0, *, chunk, n_seq, n_chunks, heads, dk, dv, n_pad):
    qk = heads * dk
    vw = heads * dv
    assert vw == 2 * qk

    def rowmap(col):
        return lambda b, c: (b * n_chunks + c, col)

    kern = functools.partial(_ret_chunk_kernel, heads=heads, dk=dk, dv=dv, n_pad=n_pad)
    return pl.pallas_call(
        kern,
        out_shape=(jax.ShapeDtypeStruct((n_seq * n_chunks * chunk, vw), BF16),
                   jax.ShapeDtypeStruct((n_seq, heads, dk, dv), F32)),
        grid=(n_seq, n_chunks),
        in_specs=[pl.BlockSpec((chunk, qk), rowmap(0)),
                  pl.BlockSpec((chunk, qk), rowmap(1)),
                  pl.BlockSpec((chunk, vw), rowmap(1)),
                  pl.BlockSpec((chunk, vw), rowmap(2)),
                  pl.BlockSpec((chunk, dk // 2), lambda b, c: (c, 0)),
                  pl.BlockSpec((chunk, dk // 2), lambda b, c: (c, 0)),
                  pl.BlockSpec((heads, dk, dv), lambda b, c: (0, 0, 0))],
        out_specs=(pl.BlockSpec((chunk, vw), lambda b, c: (b * n_chunks + c, 0)),
                   pl.BlockSpec((1, heads, dk, dv), lambda b, c: (b, 0, 0, 0))),
        scratch_shapes=[pltpu.VMEM((heads, dk, dv), F32)],
        compiler_params=_cparams("parallel", "arbitrary"),
        name="ret_chunks",
    )(proj, proj, proj, proj, cos, sin, s0)


def _ret_step_kernel(q_ref, k_ref, v_ref, gr_ref, cos_ref, sin_ref, s_ref, *rest, heads, dk, dv, nb):
    og_ref, sout_ref = rest[-2:]
    half = dk // 2
    cos = cos_ref[...]
    sin = sin_ref[...]
    base = pl.program_id(0) * nb
    pad_rows = 16
    lane_w = 128
    sel_r = lax.broadcasted_iota(jnp.int32, (pad_rows, 2 * lane_w), 0)
    sel_c = lax.broadcasted_iota(jnp.int32, (pad_rows, 2 * lane_w), 1)
    spread = jnp.where((sel_r == 0) & (sel_c < lane_w) | (sel_r == 1) & (sel_c >= lane_w), 1.0, 0.0).astype(BF16)
    row_id = lax.broadcasted_iota(jnp.int32, (pad_rows, 1), 0)

    def rope(x):
        x1 = x[:, :half]
        x2 = x[:, half:]
        return jnp.concatenate([x1 * cos - x2 * sin, x2 * cos + x1 * sin], axis=1)

    for i in range(nb):
        n = base + i
        q_row, k_row, v_row, g_row = (ref[pl.ds(n, 1), :] for ref in (q_ref, k_ref, v_ref, gr_ref))
        o_parts = []
        for h in range(heads):
            gamma = 1.0 - 2.0 ** (-5.0 - h)
            qs = slice(h * dk, (h + 1) * dk)
            vs = slice(h * dv, (h + 1) * dv)
            q = rope(q_row[:, qs])
            k = rope(k_row[:, qs]) * (dk ** -0.5)
            v = v_row[:, vs]
            kq_rows = jnp.where(row_id == 0, jnp.broadcast_to(k, (pad_rows, dk)),
                                jnp.where(row_id == 1, jnp.broadcast_to(q, (pad_rows, dk)), 0.0))
            cols = _dot_tn(_bf(kq_rows), spread)
            k_col, q_col = cols[:, :lane_w], cols[:, lane_w:]
            o_tiles = []
            for t in range(dv // lane_w):
                ts = slice(t * lane_w, (t + 1) * lane_w)
                s_new = gamma * s_ref[0, i, h, :, ts] + k_col * v[:, ts]
                sout_ref[0, i, h, :, ts] = s_new
                o_tiles.append(jnp.sum(q_col * s_new, axis=0, keepdims=True))
            o = jnp.concatenate(o_tiles, axis=1)
            o = o * lax.rsqrt(jnp.mean(o * o, axis=-1, keepdims=True) + NORM_EPS)
            g = g_row[:, vs]
            o_parts.append(o * (g * _sigmoid(g)))
        og_ref[pl.ds(n, 1), :] = jnp.concatenate(o_parts, axis=1)


def _stacked_alias(stacked_out, n_inputs):
    if stacked_out is None:
        return [], [], {}
    return [stacked_out], [pl.BlockSpec(memory_space=pl.ANY)], {n_inputs: 1}


def _ret_step(q, k, v, gr, cos, sin, states, stacked_out, layer, *, heads, dk, dv, nb):
    n_seq = q.shape[0]
    full = lambda a: pl.BlockSpec(a.shape, lambda i: (0,) * a.ndim)
    kern = functools.partial(_ret_step_kernel, heads=heads, dk=dk, dv=dv, nb=nb)
    st_spec = pl.BlockSpec((1, nb, heads, dk, dv), lambda i: (layer, i, 0, 0, 0))
    extra, extra_specs, aliases = _stacked_alias(stacked_out, 7)
    return pl.pallas_call(
        kern,
        out_shape=(jax.ShapeDtypeStruct((n_seq, heads * dv), F32),
                   jax.ShapeDtypeStruct(states.shape, F32)),
        grid=(n_seq // nb,),
        in_specs=[full(q), full(k), full(v), full(gr), full(cos), full(sin), st_spec] + extra_specs,
        out_specs=(pl.BlockSpec((n_seq, heads * dv), lambda i: (0, 0)), st_spec),
        input_output_aliases=aliases,
        compiler_params=_cparams("arbitrary"),
        name="ret_step",
    )(q, k, v, gr, cos, sin, states, *extra)


def _proj_prep_kernel(*refs, layer0, small, tm, rc, hd, lw_, la_, seq_tiles, n_pad, n_head):
    (x_ref, gain_ref, wb_ref, lead_ref, mu_ref, w0_ref, w2_ref, a0_ref, a2_ref, g2_ref, kkp_ref, kap_ref,
     v0_ref, v1_ref, v2_ref, vf_ref, r_o, lw_o, k_o, v_o, kk_o, b_o, g_o, rw_o) = refs[:24]
    rw = _dot(_bf(_rms(x_ref[...], gain_ref[...])), wb_ref[...])
    rolled = pltpu.roll(rw, 1, 0)
    if small:
        row = lax.broadcasted_iota(jnp.int32, (tm, 1), 0)
        prev = jnp.where((row <= n_pad) | (row >= n_head), lead_ref[...], rolled)
        rw_o[...] = rw
    else:
        carry_ref = refs[24]
        is_start = (pl.program_id(0) % seq_tiles) == 0
        first = jnp.where(is_start, lead_ref[...], carry_ref[...])
        row0 = lax.broadcasted_iota(jnp.int32, (8, 1), 0) == 0
        prev = jnp.concatenate([jnp.where(row0, first, rolled[:8]), rolled[8:]], axis=0)
        carry_ref[...] = rw[tm - 1:tm]
        rw_o[0] = rw[tm - 1:tm]
    z = rw + (prev - rw) * mu_ref[...]

    z_l = z[:, 3 * rc:]
    wd = z_l[:, :lw_]
    ad = z_l[:, lw_:lw_ + la_]
    gd = z_l[:, lw_ + la_:]
    w_in = w0_ref[...] + _dot(_bf(jnp.tanh(wd)), w2_ref[...])
    lw_o[...] = (-math.exp(-0.5)) * _sigmoid(w_in)
    a = _sigmoid(a0_ref[...] + _dot(_bf(ad), a2_ref[...]))
    g_o[...] = _bf(_dot(_bf(_sigmoid(gd)), g2_ref[...]))

    r_o[...] = _bf(z[:, :rc])

    z_k = z[:, rc:2 * rc]
    kk = z_k * kkp_ref[...]
    ones = _group_ones(hd)
    kk = kk * lax.rsqrt(jnp.maximum(_group_sum(kk * kk, ones), KK_EPS * KK_EPS))
    kk_o[...] = _bf(kk)
    b_o[...] = _bf(kk * a)
    k_o[...] = _bf(z_k * (1.0 + (a - 1.0) * kap_ref[...]))

    z_v = z[:, 2 * rc:3 * rc]
    if layer0:
        v_o[...] = _bf(z_v)
    else:
        lora = _dot(_bf(_dot(_bf(z_v), v1_ref[...])), v2_ref[...])
        v_o[...] = _bf(z_v + (vf_ref[...].astype(F32) - z_v) * _sigmoid(v0_ref[...] + lora))


def _proj_prep(x, gain, wb, layer, lead, p, v_first, *, layer0, small, tm, seq_rows, hd, n_pad=0, n_head=0):
    rows, d = x.shape
    width = wb.shape[2]
    rc = p["w0"].shape[1]
    lw_ = p["w2"].shape[0]
    la_ = p["a2"].shape[0]
    n_tiles = rows // tm
    assert (n_tiles == 1) if small else (tm % 8 == 0 and seq_rows % tm == 0)
    tile = lambda w: pl.BlockSpec((tm, w), lambda i: (i, 0))
    const = lambda a: pl.BlockSpec(a.shape, lambda i: (0,) * a.ndim)
    params = [p["mu"], p["w0"], p["w2"], p["a0"], p["a2"], p["g2"], p["kk"], p["ka"], p["v0"], p["v1"], p["v2"]]
    kern = functools.partial(_proj_prep_kernel, layer0=layer0, small=small, tm=tm, rc=rc, hd=hd, lw_=lw_,
                             la_=la_, seq_tiles=max(seq_rows // tm, 1), n_pad=n_pad, n_head=n_head)
    out_bf = jax.ShapeDtypeStruct((rows, rc), BF16)
    if small:
        rw_shape, rw_spec, scratch = (rows, width), tile(width), []
    else:
        rw_shape, rw_spec = (n_tiles, 1, width), pl.BlockSpec((1, 1, width), lambda i: (i, 0, 0))
        scratch = [pltpu.VMEM((1, width), F32)]
    return pl.pallas_call(
        kern,
        out_shape=(out_bf, jax.ShapeDtypeStruct((rows, rc), F32), out_bf, out_bf, out_bf, out_bf, out_bf,
                   jax.ShapeDtypeStruct(rw_shape, F32)),
        grid=(n_tiles,),
        in_specs=[tile(d), const(gain),
                  pl.BlockSpec((None, d, width), lambda i: (layer, 0, 0), pipeline_mode=pl.Buffered(1)),
                  tile(width) if small else const(lead)] + [const(a) for a in params] + [tile(rc)],
        out_specs=tuple(tile(rc) for _ in range(7)) + (rw_spec,),
        scratch_shapes=scratch,
        compiler_params=_cparams("arbitrary"),
        name="proj_prep",
    )(x, gain, wb, lead, *params, v_first)


def _wkv_chunk_kernel(r_ref, lw_ref, k_ref, v_ref, kk_ref, b_ref, s0_ref, y_ref, sout_ref, s_scr,
                      *, heads, hd, n_pad):
    c = pl.program_id(1)
    n_chunks = pl.num_programs(1)
    C = CHUNK

    @pl.when(c == 0)
    def _():
        s_scr[...] = s0_ref[...]

    ti = lax.broadcasted_iota(jnp.int32, (C, C), 0)
    tj = lax.broadcasted_iota(jnp.int32, (C, C), 1)
    lower = ti >= tj
    strict = ti > tj
    tril = jnp.where(lower, 1.0, 0.0).astype(BF16)

    lw = lw_ref[...]
    hi = _bf(lw)
    r1 = lw - hi.astype(F32)
    mid = _bf(r1)
    lo = _bf(r1 - mid.astype(F32))
    cum = _dot(tril, hi) + _dot(tril, mid) + _dot(tril, lo)
    e_in = jnp.exp(cum)
    e_inv = jnp.exp(-cum)
    e_prev = jnp.exp(cum - lw)
    total = cum[C - 1:C, :]
    e_tail = jnp.exp(total - cum)
    p_end = jnp.exp(total)

    kk = kk_ref[...].astype(F32)
    bb = b_ref[...].astype(F32)
    kx = k_ref[...].astype(F32)
    vx = v_ref[...]
    if n_pad:
        rowm = lax.broadcasted_iota(jnp.int32, kk.shape, 0) >= n_pad
        kk = jnp.where(rowm, kk, 0.0)
        bb = jnp.where(rowm, bb, 0.0)
        kx = jnp.where(rowm, kx, 0.0)
        vx = jnp.where(rowm, vx, jnp.zeros_like(vx))
    a_t = _bf(-kk * e_prev)
    b_t = _bf(bb * e_inv)
    k_t = _bf(kx * e_inv)
    r_t = _bf(r_ref[...].astype(F32) * e_in)
    b_hat = _bf(bb * e_tail)
    k_hat = _bf(kx * e_tail)

    hr = range(heads)
    hs = [slice(h * hd, (h + 1) * hd) for h in hr]
    ti2 = lax.broadcasted_iota(jnp.int32, (C, 2 * C), 0)
    tj2 = lax.broadcasted_iota(jnp.int32, (C, 2 * C), 1) & (C - 1)
    strict2 = ti2 > tj2
    lower2 = ti2 >= tj2
    s0 = [s_scr[h] for h in hr]
    s0b = [_bf(s) for s in s0]
    s4 = [_dot_nt(jnp.concatenate([a_t[:, hs[h]], r_t[:, hs[h]]], axis=0),
                  jnp.concatenate([b_t[:, hs[h]], k_t[:, hs[h]]], axis=0)) for h in hr]
    top = [jnp.where(strict2, s[:C], 0.0) for s in s4]
    bot = [_bf(jnp.where(lower2, s[C:], 0.0)) for s in s4]
    m = [t[:, :C] for t in top]
    x = [_dot_nt(a_t[:, hs[h]], s0b[h]) + _dot(_bf(top[h][:, C:]), vx[:, hs[h]]) for h in hr]
    for _ in range(5):
        mb = [_bf(v) for v in m]
        wide = [_dot(mb[h], jnp.concatenate([mb[h], _bf(x[h])], axis=1)) for h in hr]
        m = [w[:, :C] for w in wide]
        x = [x[h] + wide[h][:, C:] for h in hr]
    u = [x[h] + _dot(_bf(m[h]), _bf(x[h])) for h in hr]
    uv = [jnp.concatenate([_bf(u[h]), vx[:, hs[h]]], axis=0) for h in hr]
    for h in hr:
        y_ref[:, hs[h]] = _dot_nt(r_t[:, hs[h]], s0b[h]) + _dot(bot[h], uv[h])
    for h in hr:
        s_scr[h] = s0[h] * p_end[:, hs[h]] + _dot_tn(
            uv[h], jnp.concatenate([b_hat[:, hs[h]], k_hat[:, hs[h]]], axis=0))

    @pl.when(c == n_chunks - 1)
    def _():
        sout_ref[0] = s_scr[...]


def _wkv_chunks(r, lw, k, v, kk, b, s0, *, row_block0, n_seq, n_chunks, heads, hd, n_pad):
    rc = heads * hd
    tile = pl.BlockSpec((CHUNK, rc), lambda i, c: (row_block0 + i * n_chunks + c, 0))
    kern = functools.partial(_wkv_chunk_kernel, heads=heads, hd=hd, n_pad=n_pad)
    return pl.pallas_call(
        kern,
        out_shape=(jax.ShapeDtypeStruct((n_seq * n_chunks * CHUNK, rc), F32),
                   jax.ShapeDtypeStruct((n_seq, heads, hd, hd), F32)),
        grid=(n_seq, n_chunks),
        in_specs=[tile] * 6 + [pl.BlockSpec((heads, hd, hd), lambda i, c: (0, 0, 0))],
        out_specs=(pl.BlockSpec((CHUNK, rc), lambda i, c: (i * n_chunks + c, 0)),
                   pl.BlockSpec((1, heads, hd, hd), lambda i, c: (i, 0, 0, 0))),
        scratch_shapes=[pltpu.VMEM((heads, hd, hd), F32)],
        compiler_params=_cparams("parallel", "arbitrary"),
        name="wkv_chunks",
    )(r, lw, k, v, kk, b, s0)


def _wkv_step_kernel(r_ref, lw_ref, k_ref, v_ref, kk_ref, b_ref, s_ref, *rest, hd, slab):
    y_ref, sout_ref = rest[-2:]
    w = jnp.exp(lw_ref[...])
    kk, bb, kx, rx, vx = kk_ref[...], b_ref[...], k_ref[...], r_ref[...], v_ref[...]
    per_slab = slab // hd
    ys = []
    for j in range(hd // per_slab):
        cols = slice(j * slab, (j + 1) * slab)
        st = s_ref[0, :, cols].T
        outs = []
        for t in range(per_slab):
            vi = j * per_slab + t
            s = st[t * hd:(t + 1) * hd, :]
            s_kk = jnp.sum(s * kk, axis=0, keepdims=True)
            s_new = s * w - s_kk * bb + vx[vi:vi + 1, :] * kx
            ys.append(jnp.sum(s_new * rx, axis=0, keepdims=True))
            outs.append(s_new)
        sout_ref[0, :, cols] = jnp.concatenate(outs, axis=0).T
    y_ref[...] = jnp.concatenate(ys, axis=0)


def _wkv_step(r, lw, k, v, kk, b, states, stacked_out, layer, *, heads, hd):
    rc, n_seq = r.shape
    slab = max(hd, 128)
    vec = pl.BlockSpec((hd, n_seq), lambda h: (h, 0))
    st_spec = pl.BlockSpec((1, n_seq, hd * hd), lambda h: (layer, 0, h))
    kern = functools.partial(_wkv_step_kernel, hd=hd, slab=slab)
    extra, extra_specs, aliases = _stacked_alias(stacked_out, 7)
    return pl.pallas_call(
        kern,
        out_shape=(jax.ShapeDtypeStruct((rc, n_seq), F32), jax.ShapeDtypeStruct(states.shape, F32)),
        grid=(heads,),
        in_specs=[vec] * 6 + [st_spec] + extra_specs,
        out_specs=(vec, st_spec),
        input_output_aliases=aliases,
        compiler_params=_cparams("parallel"),
        name="wkv_step",
    )(r, lw, k, v, kk, b, states, *extra)


def _mix_kernel(og_ref, y_ref, r_ref, k_ref, v_ref, g_ref, gates_ref, x_ref,
                wret_ref, wrw_ref, wout_ref, lnw_ref, lnb_ref, rk_ref, gain_ref, o_ref, *, hd, d):
    y_ret = _dot(og_ref[...], wret_ref[...])
    ones = _group_ones(hd)
    y = y_ref[...]
    inv_n = 1.0 / hd
    mean = _group_sum(y, ones, split=False) * inv_n
    dlt = y - mean
    var = _group_sum(dlt * dlt, ones) * inv_n
    yn = dlt * lax.rsqrt(var + LNX_EPS) * lnw_ref[...] + lnb_ref[...]
    rkk = r_ref[...].astype(F32) * k_ref[...].astype(F32) * rk_ref[...]
    bonus = _group_sum(rkk, ones, split=False) * v_ref[...].astype(F32)
    y_rw = _dot(_bf((yn + bonus) * g_ref[...].astype(F32)), wrw_ref[...])
    gates = gates_ref[...].astype(F32)
    mix = _sigmoid(gates[:, :d]) * y_ret + _sigmoid(gates[:, d:]) * y_rw
    o_ref[...] = x_ref[...] + _rms(_dot(_bf(mix), wout_ref[...]), gain_ref[...])


def _mix(og, y, r, k, v, g, proj, x, w_ret, w_rw, w_out, lnw, lnb, rk, gain, *, layer, tm, hd, gate_block):
    rows, d = x.shape
    tile = lambda a: pl.BlockSpec((tm, a.shape[1]), lambda i: (i, 0))
    const = lambda a: pl.BlockSpec(a.shape, lambda i: (0,) * a.ndim)
    stacked = lambda a: pl.BlockSpec((None,) + a.shape[1:], lambda i: (layer, 0, 0),
                                     pipeline_mode=pl.Buffered(1))
    kern = functools.partial(_mix_kernel, hd=hd, d=d)
    return pl.pallas_call(
        kern,
        out_shape=jax.ShapeDtypeStruct((rows, d), F32),
        grid=(rows // tm,),
        in_specs=[tile(og), tile(y), tile(r), tile(k), tile(v), tile(g),
                  pl.BlockSpec((tm, 2 * d), lambda i: (i, gate_block)), tile(x),
                  stacked(w_ret), stacked(w_rw), stacked(w_out), const(lnw), const(lnb), const(rk), const(gain)],
        out_specs=pl.BlockSpec((tm, d), lambda i: (i, 0)),
        compiler_params=_cparams("parallel"),
        name="mix",
    )(og, y, r, k, v, g, proj, x, w_ret, w_rw, w_out, lnw, lnb, rk, gain)


def _ffn_kernel(x_ref, g2_ref, g3_ref, w1_ref, w3_ref, w2_ref, o_ref, hn_ref, acc_ref):
    j = pl.program_id(1)

    @pl.when(j == 0)
    def _():
        hn_ref[...] = _bf(_rms(x_ref[...], g2_ref[...]))
        acc_ref[...] = jnp.zeros_like(acc_ref)

    tm = hn_ref.shape[0]
    sub = FFN_SUB_ROWS if tm % FFN_SUB_ROWS == 0 else tm
    blocks = [slice(r, r + sub) for r in range(0, tm, sub)]

    def up(rows):
        hn = hn_ref[rows, :]
        return _dot(hn, w1_ref[...]), _dot(hn, w3_ref[...])

    def down(rows, ab):
        a, b = ab
        acc_ref[rows, :] += _dot(_bf((a * _sigmoid(a)) * b), w2_ref[...])

    pending = up(blocks[0])
    for r in range(1, len(blocks)):
        nxt = up(blocks[r])
        down(blocks[r - 1], pending)
        pending = nxt
    down(blocks[-1], pending)

    @pl.when(j == pl.num_programs(1) - 1)
    def _():
        o_ref[...] = x_ref[...] + _rms(acc_ref[...], g3_ref[...])


def _ffn(x, gain2, gain3, w1, w3, w2, *, layer, tm, tf):
    rows, d = x.shape
    f = w1.shape[2]
    mode = dict(pipeline_mode=pl.Buffered(1)) if tf == f else {}
    return pl.pallas_call(
        _ffn_kernel,
        out_shape=jax.ShapeDtypeStruct((rows, d), F32),
        grid=(rows // tm, f // tf),
        in_specs=[pl.BlockSpec((tm, d), lambda i, j: (i, 0)),
                  pl.BlockSpec((1, d), lambda i, j: (0, 0)),
                  pl.BlockSpec((1, d), lambda i, j: (0, 0)),
                  pl.BlockSpec((None, d, tf), lambda i, j: (layer, 0, j), **mode),
                  pl.BlockSpec((None, d, tf), lambda i, j: (layer, 0, j), **mode),
                  pl.BlockSpec((None, tf, d), lambda i, j: (layer, j, 0), **mode)],
        out_specs=pl.BlockSpec((tm, d), lambda i, j: (i, 0)),
        scratch_shapes=[pltpu.VMEM((tm, d), BF16), pltpu.VMEM((tm, d), F32)],
        compiler_params=_cparams("parallel", "arbitrary"),
        name="ffn",
    )(x, gain2, gain3, w1, w3, w2)


def _pick(n, prefs):
    for t in prefs:
        if n % t == 0:
            return t
    return n


def _rope_tables(pos, half):
    inv_freq = 1.0 / (ROPE_BASE ** (jnp.arange(half, dtype=F32) / half))
    ang = pos.astype(F32)[:, None] * inv_freq[None, :]
    return jnp.cos(ang), jnp.sin(ang)


def kernel(x_prompt, x_sample, state_ret, state_wkv, state_shift, meta_tokens, norm_gain, w_in, w_ret_out,
           w_rwkv_out, w_out, rwkv_mu, rwkv_w0, rwkv_w2, rwkv_a0, rwkv_a2, rwkv_g2, rwkv_kk, rwkv_ka, rwkv_rk,
           rwkv_lnx_w, rwkv_lnx_b, rwkv_v0, rwkv_v1, rwkv_v2, ffn_w1, ffn_w3, ffn_w2):
    bp, sp, d = x_prompt.shape
    ns = x_sample.shape[0]
    depth = w_in.shape[0]
    n_meta = meta_tokens.shape[0]
    _, _, rh, dk, dv = state_ret.shape
    _, _, wh, hd, _ = state_wkv.shape
    qk, vw, rc = rh * dk, rh * dv, wh * hd
    rw_w = state_shift.shape[-1]
    assert x_sample.shape[1] == 1 and n_meta <= CHUNK and sp % CHUNK == 0
    assert vw == 2 * qk and d == qk and rc == d and hd & (hd - 1) == 0 and MXU_TILE % hd == 0
    n_chunks = sp // CHUNK
    n_pad = CHUNK - n_meta
    main_rows = bp * sp
    small_rows = CHUNK + ns
    half = dk // 2

    xm = x_prompt.reshape(main_rows, d)
    xs = jnp.concatenate([jnp.zeros((n_pad, d), F32), meta_tokens.astype(F32), x_sample.reshape(ns, d)], axis=0)

    cos_m, sin_m = _rope_tables(n_meta + jnp.arange(sp), half)
    cos_t, sin_t = _rope_tables(jnp.arange(CHUNK) - n_pad, half)
    cos_s, sin_s = _rope_tables(jnp.full((1,), PAST_LEN), half)

    split = 2 * qk + 2 * vw
    wa = _bf(jnp.concatenate([w_in[:, :, :split], w_in[:, :, split + rw_w:]], axis=-1))
    wb = _bf(w_in[:, :, split:split + rw_w])
    gate_block = split // (2 * d)
    assert gate_block * 2 * d == split

    tm_m = _pick(main_rows, (1024, 512, 256, 128, 64))
    tm_mix = _pick(main_rows, (512, 256, 128, 64))
    ret_chunk = _pick(sp, (4 * CHUNK, 2 * CHUNK, CHUNK))
    tn_a = _pick(wa.shape[-1], (2048, 1024, 512, 256))
    tf = ffn_w1.shape[-1]
    seq_tm = _pick(sp, PREP_TILES)

    w_ret_b, w_rw_b, w_out_b = _bf(w_ret_out), _bf(w_rwkv_out), _bf(w_out)
    w1_b, w3_b, w2_b = _bf(ffn_w1), _bf(ffn_w3), _bf(ffn_w2)
    ret_p, wkv_p, sh_p, sh_s = [], [], [], []
    ret_s = wkv_s = None
    wkv_flat = state_wkv.reshape(depth, ns, wh * hd * hd)
    vf_m = vf_s = None
    for l in range(depth):
        g = norm_gain[l]
        p = dict(mu=rwkv_mu[l][None], w0=rwkv_w0[l][None], w2=_bf(rwkv_w2[l]), a0=rwkv_a0[l][None],
                 a2=_bf(rwkv_a2[l]), g2=_bf(rwkv_g2[l]), kk=rwkv_kk[l][None], ka=rwkv_ka[l][None])
        lv = max(l - 1, 0)
        p.update(v0=rwkv_v0[lv][None], v1=_bf(rwkv_v1[lv]), v2=_bf(rwkv_v2[lv]))

        pa_s = _norm_proj(xs, g[0][None], wa, l, small_rows, tn_a)
        prev_s = jnp.concatenate([jnp.zeros((CHUNK, rw_w), F32), state_shift[l]], axis=0)
        if l == 0:
            vf_s = jnp.zeros((small_rows, rc), BF16)
        r_s, lw_s, k_s, v_s, kk_s, b_s, g_s, pb_sf = _proj_prep(
            xs, g[0][None], wb, l, prev_s, p, vf_s, layer0=(l == 0), small=True, tm=small_rows,
            seq_rows=small_rows, hd=hd, n_pad=n_pad, n_head=CHUNK)
        if l == 0:
            vf_s = v_s

        og_meta, s_ret_meta = _ret_chunks(pa_s, cos_t, sin_t, jnp.zeros((rh, dk, dv), F32), chunk=CHUNK,
                                          n_seq=1, n_chunks=1, heads=rh, dk=dk, dv=dv, n_pad=n_pad)
        y_meta, s_wkv_meta = _wkv_chunks(r_s, lw_s, k_s, v_s, kk_s, b_s, jnp.zeros((wh, hd, hd), F32),
                                         row_block0=0, n_seq=1, n_chunks=1, heads=wh, hd=hd, n_pad=n_pad)
        pa_sf = pa_s[CHUNK:].astype(F32)
        og_smp, ret_s = _ret_step(pa_sf[:, :qk], pa_sf[:, qk:2 * qk], pa_sf[:, 2 * qk:2 * qk + vw],
                                  pa_sf[:, 2 * qk + vw:split], cos_s, sin_s, state_ret, ret_s, l,
                                  heads=rh, dk=dk, dv=dv, nb=_pick(ns, (2, 1)))
        smp = [a[CHUNK:].astype(F32).T for a in (r_s, lw_s, k_s, v_s, kk_s, b_s)]
        y_smp, wkv_s = _wkv_step(*smp, wkv_flat, wkv_s, l, heads=wh, hd=hd)
        og_s = jnp.concatenate([og_meta, _bf(og_smp)], axis=0)
        y_s = jnp.concatenate([y_meta, y_smp.T], axis=0)
        sh_s.append(pb_sf[CHUNK:])

        pa_m = _norm_proj(xm, g[0][None], wa, l, tm_m, tn_a)
        if l == 0:
            vf_m = jnp.zeros((main_rows, rc), BF16)
        r_m, lw_m, k_m, v_m, kk_m, b_m, g_m, tails = _proj_prep(
            xm, g[0][None], wb, l, pb_sf[CHUNK - 1:CHUNK], p, vf_m, layer0=(l == 0), small=False, tm=seq_tm,
            seq_rows=sp, hd=hd)
        if l == 0:
            vf_m = v_m
        og_m, s_ret_m = _ret_chunks(pa_m, cos_m, sin_m, s_ret_meta[0], chunk=ret_chunk, n_seq=bp,
                                    n_chunks=sp // ret_chunk, heads=rh, dk=dk, dv=dv, n_pad=0)
        y_m, s_wkv_m = _wkv_chunks(r_m, lw_m, k_m, v_m, kk_m, b_m, s_wkv_meta[0], row_block0=0, n_seq=bp,
                                   n_chunks=n_chunks, heads=wh, hd=hd, n_pad=0)
        ret_p.append(s_ret_m)
        wkv_p.append(s_wkv_m)
        sh_p.append(tails.reshape(bp, sp // seq_tm, rw_w)[:, -1])

        mixw = (w_ret_b, w_rw_b, w_out_b, rwkv_lnx_w[l][None], rwkv_lnx_b[l][None], rwkv_rk[l][None], g[1][None])
        ffnw = (g[2][None], g[3][None], w1_b, w3_b, w2_b)
        mix_kw = dict(layer=l, hd=hd, gate_block=gate_block)
        xs = _mix(og_s, y_s, r_s, k_s, v_s, g_s, pa_s, xs, *mixw, tm=small_rows, **mix_kw)
        xs = _ffn(xs, *ffnw, layer=l, tm=small_rows, tf=tf)
        xm = _mix(og_m, y_m, r_m, k_m, v_m, g_m, pa_m, xm, *mixw, tm=tm_mix, **mix_kw)
        xm = _ffn(xm, *ffnw, layer=l, tm=tm_m, tf=tf)

    return (xm.reshape(bp, sp, d), xs[CHUNK:].reshape(ns, 1, d), jnp.stack(ret_p), jnp.stack(wkv_p),
            jnp.stack(sh_p), ret_s, wkv_s.reshape(state_wkv.shape), jnp.stack(sh_s))
```

```python
import functools
import math

import jax
import jax.numpy as jnp
from jax import lax
from jax.experimental import pallas as pl
from jax.experimental.pallas import tpu as pltpu

F32 = jnp.float32
BF16 = jnp.bfloat16

NORM_EPS = 1e-6
LNX_EPS = 64e-5
ROPE_BASE = 10000.0
PAST_LEN = 16384
KK_EPS = 1e-12

CHUNK = 64
MXU_TILE = 256
VMEM_LIMIT = 56 * 1024 * 1024
PREP_TILES = (512, 256, 128, 64)
FFN_SUB_ROWS = 256
WKV_PAR = 2


def _cparams(*sem):
    return pltpu.CompilerParams(dimension_semantics=sem, vmem_limit_bytes=VMEM_LIMIT)


def _dot(a, b):
    return jnp.dot(a, b, preferred_element_type=F32)


def _dot_nt(a, b):
    return lax.dot_general(a, b, (((1,), (1,)), ((), ())), preferred_element_type=F32)


def _dot_tn(a, b):
    return lax.dot_general(a, b, (((0,), (0,)), ((), ())), preferred_element_type=F32)


def _bf(x):
    return x.astype(BF16)


def _sigmoid(x):
    return 0.5 * jnp.tanh(0.5 * x) + 0.5


def _rms(x, gain):
    return x * lax.rsqrt(jnp.mean(x * x, axis=-1, keepdims=True) + NORM_EPS) * gain


def _group_ones(group):
    shift = group.bit_length() - 1
    r = lax.broadcasted_iota(jnp.int32, (MXU_TILE, MXU_TILE), 0) >> shift
    c = lax.broadcasted_iota(jnp.int32, (MXU_TILE, MXU_TILE), 1) >> shift
    return jnp.where(r == c, 1.0, 0.0).astype(BF16)


def _group_sum(x, ones, split=True):
    hi = _bf(x)
    lo = _bf(x - hi.astype(F32)) if split else None
    parts = []
    for j in range(x.shape[1] // MXU_TILE):
        sl = slice(j * MXU_TILE, (j + 1) * MXU_TILE)
        part = _dot(hi[:, sl], ones)
        parts.append(part + _dot(lo[:, sl], ones) if split else part)
    return parts[0] if len(parts) == 1 else jnp.concatenate(parts, axis=1)


def _norm_proj_kernel(x_ref, g_ref, w_ref, o_ref, xn_ref):
    @pl.when(pl.program_id(1) == 0)
    def _():
        xn_ref[...] = _bf(_rms(x_ref[...], g_ref[...]))

    o_ref[...] = _dot(xn_ref[...], w_ref[...]).astype(o_ref.dtype)


def _norm_proj(x, gain, w, layer, tm, tn):
    rows, d = x.shape
    n = w.shape[2]
    return pl.pallas_call(
        _norm_proj_kernel,
        out_shape=jax.ShapeDtypeStruct((rows, n), BF16),
        grid=(rows // tm, n // tn),
        in_specs=[pl.BlockSpec((tm, d), lambda i, j: (i, 0)),
                  pl.BlockSpec((1, d), lambda i, j: (0, 0)),
                  pl.BlockSpec((None, d, tn), lambda i, j: (layer, 0, j))],
        out_specs=pl.BlockSpec((tm, tn), lambda i, j: (i, j)),
        scratch_shapes=[pltpu.VMEM((tm, d), BF16)],
        compiler_params=_cparams("parallel", "arbitrary"),
        name="norm_proj",
    )(x, gain, w)


def _ret_chunk_kernel(q_ref, k_ref, v_ref, gr_ref, cos_ref, sin_ref, s0_ref, og_ref, sout_ref, s_scr,
                      *, heads, dk, dv, n_pad):
    c = pl.program_id(1)
    n_chunks = pl.num_programs(1)
    half = dk // 2

    @pl.when(c == 0)
    def _():
        s_scr[...] = s0_ref[...]

    chunk = q_ref.shape[0]
    cos = cos_ref[...]
    sin = sin_ref[...]
    row = lax.broadcasted_iota(jnp.int32, (chunk, half), 0)
    rowf = row.astype(F32)
    rel_i = lax.broadcasted_iota(jnp.int32, (chunk, chunk), 0)
    rel_j = lax.broadcasted_iota(jnp.int32, (chunk, chunk), 1)
    rel = (rel_i - rel_j).astype(F32)

    def rope(x):
        x1 = x[:, :half]
        x2 = x[:, half:]
        return x1 * cos - x2 * sin, x2 * cos + x1 * sin

    hr = range(heads)
    lgs = [math.log(1.0 - 2.0 ** (-5.0 - h)) for h in hr]
    vsl = [slice(h * dv, (h + 1) * dv) for h in hr]
    qb, kb, qd, kd, vhs, dmask = [], [], [], [], [], []
    for h in hr:
        lg = lgs[h]
        qs = slice(h * dk, (h + 1) * dk)
        q1, q2 = rope(q_ref[:, qs].astype(F32))
        k1, k2 = rope(k_ref[:, qs].astype(F32))
        k1 = k1 * (dk ** -0.5)
        k2 = k2 * (dk ** -0.5)
        vh = v_ref[:, vsl[h]]
        if n_pad:
            k1 = jnp.where(row >= n_pad, k1, 0.0)
            k2 = jnp.where(row >= n_pad, k2, 0.0)
            vrow = lax.broadcasted_iota(jnp.int32, (chunk, dv), 0)
            vh = jnp.where(vrow >= n_pad, vh, jnp.zeros_like(vh))
        q_decay = jnp.exp((rowf + 1.0) * lg)
        k_decay = jnp.exp((chunk - 1.0 - rowf) * lg)
        dmask.append(jnp.where(rel >= 0, jnp.exp(lg * jnp.maximum(rel, 0.0)), 0.0))
        qb.append(_bf(jnp.concatenate([q1, q2], axis=1)))
        kb.append(_bf(jnp.concatenate([k1, k2], axis=1)))
        qd.append(_bf(jnp.concatenate([q1 * q_decay, q2 * q_decay], axis=1)))
        kd.append(_bf(jnp.concatenate([k1 * k_decay, k2 * k_decay], axis=1)))
        vhs.append(vh)
    s_old = [s_scr[h] for h in hr]
    scores = [_bf(_dot_nt(qb[h], kb[h]) * dmask[h]) for h in hr]
    cross = [_dot(qd[h], _bf(s_old[h])) for h in hr]
    inner = [_dot(scores[h], vhs[h]) for h in hr]
    for h in hr:
        o = inner[h] + cross[h]
        o = o * lax.rsqrt(jnp.mean(o * o, axis=-1, keepdims=True) + NORM_EPS)
        g = gr_ref[:, vsl[h]].astype(F32)
        og_ref[:, vsl[h]] = _bf(o * (g * _sigmoid(g)))
    for h in hr:
        s_scr[h] = math.exp(chunk * lgs[h]) * s_old[h] + _dot_tn(kd[h], vhs[h])

    @pl.when(c == n_chunks - 1)
    def _():
        sout_ref[0] = s_scr[...]


def _ret_chunks(proj, cos, sin, s0, *, chunk, n_seq, n_chunks, heads, dk, dv, n_pad):
    qk = heads * dk
    vw = heads * dv
    assert vw == 2 * qk

    def rowmap(col):
        return lambda b, c: (b * n_chunks + c, col)

    kern = functools.partial(_ret_chunk_kernel, heads=heads, dk=dk, dv=dv, n_pad=n_pad)
    return pl.pallas_call(
        kern,
        out_shape=(jax.ShapeDtypeStruct((n_seq * n_chunks * chunk, vw), BF16),
                   jax.ShapeDtypeStruct((n_seq, heads, dk, dv), F32)),
        grid=(n_seq, n_chunks),
        in_specs=[pl.BlockSpec((chunk, qk), rowmap(0)),
                  pl.BlockSpec((chunk, qk), rowmap(1)),
                  pl.BlockSpec((chunk, vw), rowmap(1)),
                  pl.BlockSpec((chunk, vw), rowmap(2)),
                  pl.BlockSpec((chunk, dk // 2), lambda b, c: (c, 0)),
                  pl.BlockSpec((chunk, dk // 2), lambda b, c: (c, 0)),
                  pl.BlockSpec((heads, dk, dv), lambda b, c: (0, 0, 0))],
        out_specs=(pl.BlockSpec((chunk, vw), lambda b, c: (b * n_chunks + c, 0)),
                   pl.BlockSpec((1, heads, dk, dv), lambda b, c: (b, 0, 0, 0))),
        scratch_shapes=[pltpu.VMEM((heads, dk, dv), F32)],
        compiler_params=_cparams("parallel", "arbitrary"),
        name="ret_chunks",
    )(proj, proj, proj, proj, cos, sin, s0)


def _ret_step_kernel(q_ref, k_ref, v_ref, gr_ref, cos_ref, sin_ref, s_ref, *rest, heads, dk, dv, nb):
    og_ref, sout_ref = rest[-2:]
    half = dk // 2
    cos = cos_ref[...]
    sin = sin_ref[...]
    base = pl.program_id(0) * nb
    pad_rows = 16
    lane_w = 128
    sel_r = lax.broadcasted_iota(jnp.int32, (pad_rows, 2 * lane_w), 0)
    sel_c = lax.broadcasted_iota(jnp.int32, (pad_rows, 2 * lane_w), 1)
    spread = jnp.where((sel_r == 0) & (sel_c < lane_w) | (sel_r == 1) & (sel_c >= lane_w), 1.0, 0.0).astype(BF16)
    row_id = lax.broadcasted_iota(jnp.int32, (pad_rows, 1), 0)

    def rope(x):
        x1 = x[:, :half]
        x2 = x[:, half:]
        return jnp.concatenate([x1 * cos - x2 * sin, x2 * cos + x1 * sin], axis=1)

    for i in range(nb):
        n = base + i
        q_row, k_row, v_row, g_row = (ref[pl.ds(n, 1), :] for ref in (q_ref, k_ref, v_ref, gr_ref))
        o_parts = []
        for h in range(heads):
            gamma = 1.0 - 2.0 ** (-5.0 - h)
            qs = slice(h * dk, (h + 1) * dk)
            vs = slice(h * dv, (h + 1) * dv)
            q = rope(q_row[:, qs])
            k = rope(k_row[:, qs]) * (dk ** -0.5)
            v = v_row[:, vs]
            kq_rows = jnp.where(row_id == 0, jnp.broadcast_to(k, (pad_rows, dk)),
                                jnp.where(row_id == 1, jnp.broadcast_to(q, (pad_rows, dk)), 0.0))
            cols = _dot_tn(_bf(kq_rows), spread)
            k_col, q_col = cols[:, :lane_w], cols[:, lane_w:]
            o_tiles = []
            for t in range(dv // lane_w):
                ts = slice(t * lane_w, (t + 1) * lane_w)
                s_new = gamma * s_ref[0, i, h, :, ts] + k_col * v[:, ts]
                sout_ref[0, i, h, :, ts] = s_new
                o_tiles.append(jnp.sum(q_col * s_new, axis=0, keepdims=True))
            o = jnp.concatenate(o_tiles, axis=1)
            o = o * lax.rsqrt(jnp.mean(o * o, axis=-1, keepdims=True) + NORM_EPS)
            g = g_row[:, vs]
            o_parts.append(o * (g * _sigmoid(g)))
        og_ref[pl.ds(n, 1), :] = jnp.concatenate(o_parts, axis=1)


def _stacked_alias(stacked_out, n_inputs):
    if stacked_out is None:
        return [], [], {}
    return [stacked_out], [pl.BlockSpec(memory_space=pl.ANY)], {n_inputs: 1}


def _ret_step(q, k, v, gr, cos, sin, states, stacked_out, layer, *, heads, dk, dv, nb):
    n_seq = q.shape[0]
    full = lambda a: pl.BlockSpec(a.shape, lambda i: (0,) * a.ndim)
    kern = functools.partial(_ret_step_kernel, heads=heads, dk=dk, dv=dv, nb=nb)
    st_spec = pl.BlockSpec((1, nb, heads, dk, dv), lambda i: (layer, i, 0, 0, 0))
    extra, extra_specs, aliases = _stacked_alias(stacked_out, 7)
    return pl.pallas_call(
        kern,
        out_shape=(jax.ShapeDtypeStruct((n_seq, heads * dv), F32),
                   jax.ShapeDtypeStruct(states.shape, F32)),
        grid=(n_seq // nb,),
        in_specs=[full(q), full(k), full(v), full(gr), full(cos), full(sin), st_spec] + extra_specs,
        out_specs=(pl.BlockSpec((n_seq, heads * dv), lambda i: (0, 0)), st_spec),
        input_output_aliases=aliases,
        compiler_params=_cparams("arbitrary"),
        name="ret_step",
    )(q, k, v, gr, cos, sin, states, *extra)


def _proj_prep_kernel(*refs, layer0, small, tm, rc, hd, lw_, la_, seq_tiles, n_pad, n_head):
    (x_ref, gain_ref, wb_ref, lead_ref, mu_ref, w0_ref, w2_ref, a0_ref, a2_ref, g2_ref, kkp_ref, kap_ref,
     v0_ref, v1_ref, v2_ref, vf_ref, r_o, lw_o, k_o, v_o, kk_o, b_o, g_o, rw_o) = refs[:24]
    rw = _dot(_bf(_rms(x_ref[...], gain_ref[...])), wb_ref[...])
    rolled = pltpu.roll(rw, 1, 0)
    if small:
        row = lax.broadcasted_iota(jnp.int32, (tm, 1), 0)
        prev = jnp.where((row <= n_pad) | (row >= n_head), lead_ref[...], rolled)
        rw_o[...] = rw
    else:
        carry_ref = refs[24]
        is_start = (pl.program_id(0) % seq_tiles) == 0
        first = jnp.where(is_start, lead_ref[...], carry_ref[...])
        row0 = lax.broadcasted_iota(jnp.int32, (8, 1), 0) == 0
        prev = jnp.concatenate([jnp.where(row0, first, rolled[:8]), rolled[8:]], axis=0)
        carry_ref[...] = rw[tm - 1:tm]
        rw_o[0] = rw[tm - 1:tm]
    z = rw + (prev - rw) * mu_ref[...]

    z_l = z[:, 3 * rc:]
    wd = z_l[:, :lw_]
    ad = z_l[:, lw_:lw_ + la_]
    gd = z_l[:, lw_ + la_:]
    w_in = w0_ref[...] + _dot(_bf(jnp.tanh(wd)), w2_ref[...])
    lw_o[...] = (-math.exp(-0.5)) * _sigmoid(w_in)
    a = _sigmoid(a0_ref[...] + _dot(_bf(ad), a2_ref[...]))
    g_o[...] = _bf(_dot(_bf(_sigmoid(gd)), g2_ref[...]))

    r_o[...] = _bf(z[:, :rc])

    z_k = z[:, rc:2 * rc]
    kk = z_k * kkp_ref[...]
    ones = _group_ones(hd)
    kk = kk * lax.rsqrt(jnp.maximum(_group_sum(kk * kk, ones, split=False), KK_EPS * KK_EPS))
    kk_o[...] = _bf(kk)
    b_o[...] = _bf(kk * a)
    k_o[...] = _bf(z_k * (1.0 + (a - 1.0) * kap_ref[...]))

    z_v = z[:, 2 * rc:3 * rc]
    if layer0:
        v_o[...] = _bf(z_v)
    else:
        lora = _dot(_bf(_dot(_bf(z_v), v1_ref[...])), v2_ref[...])
        v_o[...] = _bf(z_v + (vf_ref[...].astype(F32) - z_v) * _sigmoid(v0_ref[...] + lora))


def _proj_prep(x, gain, wb, layer, lead, p, v_first, *, layer0, small, tm, seq_rows, hd, n_pad=0, n_head=0):
    rows, d = x.shape
    width = wb.shape[2]
    rc = p["w0"].shape[1]
    lw_ = p["w2"].shape[0]
    la_ = p["a2"].shape[0]
    n_tiles = rows // tm
    assert (n_tiles == 1) if small else (tm % 8 == 0 and seq_rows % tm == 0)
    tile = lambda w: pl.BlockSpec((tm, w), lambda i: (i, 0))
    const = lambda a: pl.BlockSpec(a.shape, lambda i: (0,) * a.ndim)
    params = [p["mu"], p["w0"], p["w2"], p["a0"], p["a2"], p["g2"], p["kk"], p["ka"], p["v0"], p["v1"], p["v2"]]
    kern = functools.partial(_proj_prep_kernel, layer0=layer0, small=small, tm=tm, rc=rc, hd=hd, lw_=lw_,
                             la_=la_, seq_tiles=max(seq_rows // tm, 1), n_pad=n_pad, n_head=n_head)
    out_bf = jax.ShapeDtypeStruct((rows, rc), BF16)
    if small:
        rw_shape, rw_spec, scratch = (rows, width), tile(width), []
    else:
        rw_shape, rw_spec = (n_tiles, 1, width), pl.BlockSpec((1, 1, width), lambda i: (i, 0, 0))
        scratch = [pltpu.VMEM((1, width), F32)]
    return pl.pallas_call(
        kern,
        out_shape=(out_bf, jax.ShapeDtypeStruct((rows, rc), F32), out_bf, out_bf, out_bf, out_bf, out_bf,
                   jax.ShapeDtypeStruct(rw_shape, F32)),
        grid=(n_tiles,),
        in_specs=[tile(d), const(gain),
                  pl.BlockSpec((None, d, width), lambda i: (layer, 0, 0), pipeline_mode=pl.Buffered(1)),
                  tile(width) if small else const(lead)] + [const(a) for a in params] + [tile(rc)],
        out_specs=tuple(tile(rc) for _ in range(7)) + (rw_spec,),
        scratch_shapes=scratch,
        compiler_params=_cparams("arbitrary"),
        name="proj_prep",
    )(x, gain, wb, lead, *params, v_first)


def _wkv_chunk_kernel(r_ref, lw_ref, k_ref, v_ref, kk_ref, b_ref, s0_ref, y_ref, sout_ref, s_scr,
                      *, heads, hd, n_pad, n_par):
    c = pl.program_id(1)
    n_chunks = pl.num_programs(1)
    C = CHUNK

    @pl.when(c == 0)
    def _():
        for s in range(n_par):
            s_scr[s] = s0_ref[...]

    tril = jnp.where(lax.broadcasted_iota(jnp.int32, (C, C), 0) >= lax.broadcasted_iota(jnp.int32, (C, C), 1),
                     1.0, 0.0).astype(BF16)
    ti2 = lax.broadcasted_iota(jnp.int32, (C, 2 * C), 0)
    tj2 = lax.broadcasted_iota(jnp.int32, (C, 2 * C), 1)
    right = tj2 >= C
    strict2 = ti2 > (tj2 & (C - 1))
    lower2 = ti2 >= (tj2 & (C - 1))

    def scaled(s):
        lw = lw_ref[s]
        hi = _bf(lw)
        r1 = lw - hi.astype(F32)
        mid = _bf(r1)
        lo = _bf(r1 - mid.astype(F32))
        cum = _dot(tril, hi) + _dot(tril, mid) + _dot(tril, lo)
        total = cum[C - 1:C, :]
        e_inv = jnp.exp(-cum)
        e_tail = jnp.exp(total - cum)
        kk = kk_ref[s].astype(F32)
        bb = b_ref[s].astype(F32)
        kx = k_ref[s].astype(F32)
        vx = v_ref[s]
        if n_pad:
            rowm = lax.broadcasted_iota(jnp.int32, kk.shape, 0) >= n_pad
            kk = jnp.where(rowm, kk, 0.0)
            bb = jnp.where(rowm, bb, 0.0)
            kx = jnp.where(rowm, kx, 0.0)
            vx = jnp.where(rowm, vx, jnp.zeros_like(vx))
        return dict(a=_bf(-kk * jnp.exp(cum - lw)),
                    b=_bf(bb * e_inv), k=_bf(kx * e_inv), r=_bf(r_ref[s].astype(F32) * jnp.exp(cum)),
                    b_hat=_bf(bb * e_tail), k_hat=_bf(kx * e_tail), v=vx, p_end=jnp.exp(total))

    seqs = [scaled(s) for s in range(n_par)]
    pairs = [(s, h) for s in range(n_par) for h in range(heads)]
    col = lambda s, h, name: seqs[s][name][:, h * hd:(h + 1) * hd]
    s0 = [s_scr[s, h] for s, h in pairs]
    s0b = [_bf(v) for v in s0]
    ar = [jnp.concatenate([col(s, h, "a"), col(s, h, "r")], axis=0) for s, h in pairs]
    s4 = [_dot_nt(ar[i], jnp.concatenate([col(s, h, "b"), col(s, h, "k")], axis=0))
          for i, (s, h) in enumerate(pairs)]
    from_state = [_dot_nt(ar[i], s0b[i]) for i in range(len(pairs))]
    top = [jnp.where(strict2, v[:C], 0.0) for v in s4]
    bot = [_bf(jnp.where(lower2, v[C:], 0.0)) for v in s4]
    x0 = [from_state[i][:C] + _dot(_bf(top[i][:, C:]), col(s, h, "v")) for i, (s, h) in enumerate(pairs)]
    z = [jnp.concatenate([top[i][:, :C], x0[i]], axis=1) for i in range(len(pairs))]
    for _ in range(6):
        zb = [_bf(v) for v in z]
        z = [_dot(zb[i][:, :C], zb[i]) + jnp.where(right, z[i], 0.0) for i in range(len(pairs))]
    uv = [jnp.concatenate([_bf(z[i][:, C:]), col(s, h, "v")], axis=0) for i, (s, h) in enumerate(pairs)]
    for i, (s, h) in enumerate(pairs):
        y_ref[s, :, h * hd:(h + 1) * hd] = from_state[i][C:] + _dot(bot[i], uv[i])
    for i, (s, h) in enumerate(pairs):
        s_scr[s, h] = s0[i] * seqs[s]["p_end"][:, h * hd:(h + 1) * hd] + _dot_tn(
            uv[i], jnp.concatenate([col(s, h, "b_hat"), col(s, h, "k_hat")], axis=0))

    @pl.when(c == n_chunks - 1)
    def _():
        sout_ref[...] = s_scr[...]


def _wkv_chunks(r, lw, k, v, kk, b, s0, *, n_par, heads, hd, n_pad):
    n_seq, tokens, rc = r.shape
    n_chunks = tokens // CHUNK
    tile = pl.BlockSpec((n_par, CHUNK, rc), lambda i, c: (i, c, 0))
    state = pl.BlockSpec((n_par, heads, hd, hd), lambda i, c: (i, 0, 0, 0))
    kern = functools.partial(_wkv_chunk_kernel, heads=heads, hd=hd, n_pad=n_pad, n_par=n_par)
    return pl.pallas_call(
        kern,
        out_shape=(jax.ShapeDtypeStruct((n_seq, tokens, rc), F32),
                   jax.ShapeDtypeStruct((n_seq, heads, hd, hd), F32)),
        grid=(n_seq // n_par, n_chunks),
        in_specs=[tile] * 6 + [pl.BlockSpec((heads, hd, hd), lambda i, c: (0, 0, 0))],
        out_specs=(tile, state),
        scratch_shapes=[pltpu.VMEM((n_par, heads, hd, hd), F32)],
        compiler_params=_cparams("parallel", "arbitrary"),
        name="wkv_chunks",
    )(r, lw, k, v, kk, b, s0)


def _wkv_step_kernel(r_ref, lw_ref, k_ref, v_ref, kk_ref, b_ref, s_ref, *rest, hd, slab):
    y_ref, sout_ref = rest[-2:]
    w = jnp.exp(lw_ref[...])
    kk, bb, kx, rx, vx = kk_ref[...], b_ref[...], k_ref[...], r_ref[...], v_ref[...]
    per_slab = slab // hd
    ys = []
    for j in range(hd // per_slab):
        cols = slice(j * slab, (j + 1) * slab)
        st = s_ref[0, :, cols].T
        outs = []
        for t in range(per_slab):
            vi = j * per_slab + t
            s = st[t * hd:(t + 1) * hd, :]
            s_kk = jnp.sum(s * kk, axis=0, keepdims=True)
            s_new = s * w - s_kk * bb + vx[vi:vi + 1, :] * kx
            ys.append(jnp.sum(s_new * rx, axis=0, keepdims=True))
            outs.append(s_new)
        sout_ref[0, :, cols] = jnp.concatenate(outs, axis=0).T
    y_ref[...] = jnp.concatenate(ys, axis=0)


def _wkv_step(r, lw, k, v, kk, b, states, stacked_out, layer, *, heads, hd):
    rc, n_seq = r.shape
    slab = max(hd, 128)
    vec = pl.BlockSpec((hd, n_seq), lambda h: (h, 0))
    st_spec = pl.BlockSpec((1, n_seq, hd * hd), lambda h: (layer, 0, h))
    kern = functools.partial(_wkv_step_kernel, hd=hd, slab=slab)
    extra, extra_specs, aliases = _stacked_alias(stacked_out, 7)
    return pl.pallas_call(
        kern,
        out_shape=(jax.ShapeDtypeStruct((rc, n_seq), F32), jax.ShapeDtypeStruct(states.shape, F32)),
        grid=(heads,),
        in_specs=[vec] * 6 + [st_spec] + extra_specs,
        out_specs=(vec, st_spec),
        input_output_aliases=aliases,
        compiler_params=_cparams("parallel"),
        name="wkv_step",
    )(r, lw, k, v, kk, b, states, *extra)


def _mix_kernel(og_ref, y_ref, r_ref, k_ref, v_ref, g_ref, gates_ref, x_ref,
                wret_ref, wrw_ref, wout_ref, lnw_ref, lnb_ref, rk_ref, gain_ref, o_ref, *, hd, d):
    y_ret = _dot(og_ref[...], wret_ref[...])
    ones = _group_ones(hd)
    y = y_ref[...]
    inv_n = 1.0 / hd
    mean = _group_sum(y, ones, split=False) * inv_n
    dlt = y - mean
    var = _group_sum(dlt * dlt, ones) * inv_n
    yn = dlt * lax.rsqrt(var + LNX_EPS) * lnw_ref[...] + lnb_ref[...]
    rkk = r_ref[...].astype(F32) * k_ref[...].astype(F32) * rk_ref[...]
    bonus = _group_sum(rkk, ones, split=False) * v_ref[...].astype(F32)
    y_rw = _dot(_bf((yn + bonus) * g_ref[...].astype(F32)), wrw_ref[...])
    gates = gates_ref[...].astype(F32)
    mix = _sigmoid(gates[:, :d]) * y_ret + _sigmoid(gates[:, d:]) * y_rw
    o_ref[...] = x_ref[...] + _rms(_dot(_bf(mix), wout_ref[...]), gain_ref[...])


def _mix(og, y, r, k, v, g, proj, x, w_ret, w_rw, w_out, lnw, lnb, rk, gain, *, layer, tm, hd, gate_block):
    rows, d = x.shape
    tile = lambda a: pl.BlockSpec((tm, a.shape[1]), lambda i: (i, 0))
    const = lambda a: pl.BlockSpec(a.shape, lambda i: (0,) * a.ndim)
    stacked = lambda a: pl.BlockSpec((None,) + a.shape[1:], lambda i: (layer, 0, 0),
                                     pipeline_mode=pl.Buffered(1))
    kern = functools.partial(_mix_kernel, hd=hd, d=d)
    return pl.pallas_call(
        kern,
        out_shape=jax.ShapeDtypeStruct((rows, d), F32),
        grid=(rows // tm,),
        in_specs=[tile(og), tile(y), tile(r), tile(k), tile(v), tile(g),
                  pl.BlockSpec((tm, 2 * d), lambda i: (i, gate_block)), tile(x),
                  stacked(w_ret), stacked(w_rw), stacked(w_out), const(lnw), const(lnb), const(rk), const(gain)],
        out_specs=pl.BlockSpec((tm, d), lambda i: (i, 0)),
        compiler_params=_cparams("parallel"),
        name="mix",
    )(og, y, r, k, v, g, proj, x, w_ret, w_rw, w_out, lnw, lnb, rk, gain)


def _ffn_kernel(x_ref, g2_ref, g3_ref, w1_ref, w3_ref, w2_ref, o_ref, hn_ref, acc_ref):
    j = pl.program_id(1)

    @pl.when(j == 0)
    def _():
        hn_ref[...] = _bf(_rms(x_ref[...], g2_ref[...]))
        acc_ref[...] = jnp.zeros_like(acc_ref)

    tm = hn_ref.shape[0]
    sub = FFN_SUB_ROWS if tm % FFN_SUB_ROWS == 0 else tm
    blocks = [slice(r, r + sub) for r in range(0, tm, sub)]

    def up(rows):
        hn = hn_ref[rows, :]
        return _dot(hn, w1_ref[...]), _dot(hn, w3_ref[...])

    def down(rows, ab):
        a, b = ab
        acc_ref[rows, :] += _dot(_bf((a * _sigmoid(a)) * b), w2_ref[...])

    pending = up(blocks[0])
    for r in range(1, len(blocks)):
        nxt = up(blocks[r])
        down(blocks[r - 1], pending)
        pending = nxt
    down(blocks[-1], pending)

    @pl.when(j == pl.num_programs(1) - 1)
    def _():
        o_ref[...] = x_ref[...] + _rms(acc_ref[...], g3_ref[...])


def _ffn(x, gain2, gain3, w1, w3, w2, *, layer, tm, tf):
    rows, d = x.shape
    f = w1.shape[2]
    mode = dict(pipeline_mode=pl.Buffered(1)) if tf == f else {}
    return pl.pallas_call(
        _ffn_kernel,
        out_shape=jax.ShapeDtypeStruct((rows, d), F32),
        grid=(rows // tm, f // tf),
        in_specs=[pl.BlockSpec((tm, d), lambda i, j: (i, 0)),
                  pl.BlockSpec((1, d), lambda i, j: (0, 0)),
                  pl.BlockSpec((1, d), lambda i, j: (0, 0)),
                  pl.BlockSpec((None, d, tf), lambda i, j: (layer, 0, j), **mode),
                  pl.BlockSpec((None, d, tf), lambda i, j: (layer, 0, j), **mode),
                  pl.BlockSpec((None, tf, d), lambda i, j: (layer, j, 0), **mode)],
        out_specs=pl.BlockSpec((tm, d), lambda i, j: (i, 0)),
        scratch_shapes=[pltpu.VMEM((tm, d), BF16), pltpu.VMEM((tm, d), F32)],
        compiler_params=_cparams("parallel", "arbitrary"),
        name="ffn",
    )(x, gain2, gain3, w1, w3, w2)


def _pick(n, prefs):
    for t in prefs:
        if n % t == 0:
            return t
    return n


def _rope_tables(pos, half):
    inv_freq = 1.0 / (ROPE_BASE ** (jnp.arange(half, dtype=F32) / half))
    ang = pos.astype(F32)[:, None] * inv_freq[None, :]
    return jnp.cos(ang), jnp.sin(ang)


def kernel(x_prompt, x_sample, state_ret, state_wkv, state_shift, meta_tokens, norm_gain, w_in, w_ret_out,
           w_rwkv_out, w_out, rwkv_mu, rwkv_w0, rwkv_w2, rwkv_a0, rwkv_a2, rwkv_g2, rwkv_kk, rwkv_ka, rwkv_rk,
           rwkv_lnx_w, rwkv_lnx_b, rwkv_v0, rwkv_v1, rwkv_v2, ffn_w1, ffn_w3, ffn_w2):
    bp, sp, d = x_prompt.shape
    ns = x_sample.shape[0]
    depth = w_in.shape[0]
    n_meta = meta_tokens.shape[0]
    _, _, rh, dk, dv = state_ret.shape
    _, _, wh, hd, _ = state_wkv.shape
    qk, vw, rc = rh * dk, rh * dv, wh * hd
    rw_w = state_shift.shape[-1]
    assert x_sample.shape[1] == 1 and n_meta <= CHUNK and sp % CHUNK == 0
    assert vw == 2 * qk and d == qk and rc == d and hd & (hd - 1) == 0 and MXU_TILE % hd == 0
    n_chunks = sp // CHUNK
    n_pad = CHUNK - n_meta
    main_rows = bp * sp
    small_rows = CHUNK + ns
    half = dk // 2

    xm = x_prompt.reshape(main_rows, d)
    xs = jnp.concatenate([jnp.zeros((n_pad, d), F32), meta_tokens.astype(F32), x_sample.reshape(ns, d)], axis=0)

    cos_m, sin_m = _rope_tables(n_meta + jnp.arange(sp), half)
    cos_t, sin_t = _rope_tables(jnp.arange(CHUNK) - n_pad, half)
    cos_s, sin_s = _rope_tables(jnp.full((1,), PAST_LEN), half)

    split = 2 * qk + 2 * vw
    wa = jnp.concatenate([_bf(w_in[:, :, :split]), _bf(w_in[:, :, split + rw_w:])], axis=-1)
    wb = _bf(w_in[:, :, split:split + rw_w])
    gate_block = split // (2 * d)
    assert gate_block * 2 * d == split

    tm_m = _pick(main_rows, (1024, 512, 256, 128, 64))
    tm_mix = _pick(main_rows, (512, 256, 128, 64))
    ret_chunk = _pick(sp, (4 * CHUNK, 2 * CHUNK, CHUNK))
    tm_proj = _pick(main_rows, (2048, 1024, 512, 256, 128, 64))
    tn_a = _pick(wa.shape[-1], (1024, 512, 256))
    tf = ffn_w1.shape[-1]
    seq_tm = _pick(sp, PREP_TILES)

    w_ret_b, w_rw_b, w_out_b = _bf(w_ret_out), _bf(w_rwkv_out), _bf(w_out)
    w1_b, w3_b, w2_b = _bf(ffn_w1), _bf(ffn_w3), _bf(ffn_w2)
    ret_p, wkv_p, sh_p, sh_s = [], [], [], []
    ret_s = wkv_s = None
    wkv_flat = state_wkv.reshape(depth, ns, wh * hd * hd)
    vf_m = vf_s = None
    for l in range(depth):
        g = norm_gain[l]
        p = dict(mu=rwkv_mu[l][None], w0=rwkv_w0[l][None], w2=_bf(rwkv_w2[l]), a0=rwkv_a0[l][None],
                 a2=_bf(rwkv_a2[l]), g2=_bf(rwkv_g2[l]), kk=rwkv_kk[l][None], ka=rwkv_ka[l][None])
        lv = max(l - 1, 0)
        p.update(v0=rwkv_v0[lv][None], v1=_bf(rwkv_v1[lv]), v2=_bf(rwkv_v2[lv]))

        pa_s = _norm_proj(xs, g[0][None], wa, l, small_rows, tn_a)
        prev_s = jnp.concatenate([jnp.zeros((CHUNK, rw_w), F32), state_shift[l]], axis=0)
        if l == 0:
            vf_s = jnp.zeros((small_rows, rc), BF16)
        r_s, lw_s, k_s, v_s, kk_s, b_s, g_s, pb_sf = _proj_prep(
            xs, g[0][None], wb, l, prev_s, p, vf_s, layer0=(l == 0), small=True, tm=small_rows,
            seq_rows=small_rows, hd=hd, n_pad=n_pad, n_head=CHUNK)
        if l == 0:
            vf_s = v_s

        og_meta, s_ret_meta = _ret_chunks(pa_s, cos_t, sin_t, jnp.zeros((rh, dk, dv), F32), chunk=CHUNK,
                                          n_seq=1, n_chunks=1, heads=rh, dk=dk, dv=dv, n_pad=n_pad)
        meta_in = [a[:CHUNK].reshape(1, CHUNK, rc) for a in (r_s, lw_s, k_s, v_s, kk_s, b_s)]
        y_meta, s_wkv_meta = _wkv_chunks(*meta_in, jnp.zeros((wh, hd, hd), F32), n_par=1, heads=wh, hd=hd,
                                         n_pad=n_pad)
        y_meta = y_meta.reshape(CHUNK, rc)
        pa_sf = pa_s[CHUNK:].astype(F32)
        og_smp, ret_s = _ret_step(pa_sf[:, :qk], pa_sf[:, qk:2 * qk], pa_sf[:, 2 * qk:2 * qk + vw],
                                  pa_sf[:, 2 * qk + vw:split], cos_s, sin_s, state_ret, ret_s, l,
                                  heads=rh, dk=dk, dv=dv, nb=_pick(ns, (2, 1)))
        smp = [a[CHUNK:].astype(F32).T for a in (r_s, lw_s, k_s, v_s, kk_s, b_s)]
        y_smp, wkv_s = _wkv_step(*smp, wkv_flat, wkv_s, l, heads=wh, hd=hd)
        og_s = jnp.concatenate([og_meta, _bf(og_smp)], axis=0)
        y_s = jnp.concatenate([y_meta, y_smp.T], axis=0)
        sh_s.append(pb_sf[CHUNK:])

        pa_m = _norm_proj(xm, g[0][None], wa, l, tm_proj, tn_a)
        if l == 0:
            vf_m = jnp.zeros((main_rows, rc), BF16)
        r_m, lw_m, k_m, v_m, kk_m, b_m, g_m, tails = _proj_prep(
            xm, g[0][None], wb, l, pb_sf[CHUNK - 1:CHUNK], p, vf_m, layer0=(l == 0), small=False, tm=seq_tm,
            seq_rows=sp, hd=hd)
        if l == 0:
            vf_m = v_m
        og_m, s_ret_m = _ret_chunks(pa_m, cos_m, sin_m, s_ret_meta[0], chunk=ret_chunk, n_seq=bp,
                                    n_chunks=sp // ret_chunk, heads=rh, dk=dk, dv=dv, n_pad=0)
        main_in = [a.reshape(bp, sp, rc) for a in (r_m, lw_m, k_m, v_m, kk_m, b_m)]
        y_m, s_wkv_m = _wkv_chunks(*main_in, s_wkv_meta[0], n_par=_pick(bp, (WKV_PAR, 1)), heads=wh, hd=hd,
                                   n_pad=0)
        y_m = y_m.reshape(main_rows, rc)
        ret_p.append(s_ret_m)
        wkv_p.append(s_wkv_m)
        sh_p.append(tails.reshape(bp, sp // seq_tm, rw_w)[:, -1])

        mixw = (w_ret_b, w_rw_b, w_out_b, rwkv_lnx_w[l][None], rwkv_lnx_b[l][None], rwkv_rk[l][None], g[1][None])
        ffnw = (g[2][None], g[3][None], w1_b, w3_b, w2_b)
        mix_kw = dict(layer=l, hd=hd, gate_block=gate_block)
        xs = _mix(og_s, y_s, r_s, k_s, v_s, g_s, pa_s, xs, *mixw, tm=small_rows, **mix_kw)
        xs = _ffn(xs, *ffnw, layer=l, tm=small_rows, tf=tf)
        xm = _mix(og_m, y_m, r_m, k_m, v_m, g_m, pa_m, xm, *mixw, tm=tm_mix, **mix_kw)
        xm = _ffn(xm, *ffnw, layer=l, tm=tm_m, tf=tf)

    return (xm.reshape(bp, sp, d), xs[CHUNK:].reshape(ns, 1, d), jnp.stack(ret_p), jnp.stack(wkv_p),
            jnp.stack(sh_p), ret_s, wkv_s.reshape(state_wkv.shape), jnp.stack(sh_s))
```

```python
import functools
import math

import jax
import jax.numpy as jnp
from jax import lax
from jax.experimental import pallas as pl
from jax.experimental.pallas import tpu as pltpu

F32 = jnp.float32
BF16 = jnp.bfloat16

NORM_EPS = 1e-6
LNX_EPS = 64e-5
ROPE_BASE = 10000.0
PAST_LEN = 16384
KK_EPS = 1e-12

CHUNK = 64
MXU_TILE = 256
VMEM_LIMIT = 56 * 1024 * 1024
PREP_TILES = (512, 256, 128, 64)
FFN_SUB_ROWS = 256
WKV_PAR = 2


def _cparams(*sem):
    return pltpu.CompilerParams(dimension_semantics=sem, vmem_limit_bytes=VMEM_LIMIT)


def _dot(a, b):
    return jnp.dot(a, b, preferred_element_type=F32)


def _dot_nt(a, b):
    return lax.dot_general(a, b, (((1,), (1,)), ((), ())), preferred_element_type=F32)


def _dot_tn(a, b):
    return lax.dot_general(a, b, (((0,), (0,)), ((), ())), preferred_element_type=F32)


def _bf(x):
    return x.astype(BF16)


def _sigmoid(x):
    return 0.5 * jnp.tanh(0.5 * x) + 0.5


def _rms(x, gain):
    return x * lax.rsqrt(jnp.mean(x * x, axis=-1, keepdims=True) + NORM_EPS) * gain


def _group_ones(group):
    shift = group.bit_length() - 1
    r = lax.broadcasted_iota(jnp.int32, (MXU_TILE, MXU_TILE), 0) >> shift
    c = lax.broadcasted_iota(jnp.int32, (MXU_TILE, MXU_TILE), 1) >> shift
    return jnp.where(r == c, 1.0, 0.0).astype(BF16)


def _group_sum(x, ones, split=True):
    hi = _bf(x)
    lo = _bf(x - hi.astype(F32)) if split else None
    parts = []
    for j in range(x.shape[1] // MXU_TILE):
        sl = slice(j * MXU_TILE, (j + 1) * MXU_TILE)
        part = _dot(hi[:, sl], ones)
        parts.append(part + _dot(lo[:, sl], ones) if split else part)
    return parts[0] if len(parts) == 1 else jnp.concatenate(parts, axis=1)


def _norm_proj_kernel(x_ref, g_ref, w_ref, o_ref, xn_ref):
    @pl.when(pl.program_id(1) == 0)
    def _():
        xn_ref[...] = _bf(_rms(x_ref[...], g_ref[...]))

    o_ref[...] = _dot(xn_ref[...], w_ref[...]).astype(o_ref.dtype)


def _norm_proj(x, gain, w, layer, tm, tn):
    rows, d = x.shape
    n = w.shape[2]
    return pl.pallas_call(
        _norm_proj_kernel,
        out_shape=jax.ShapeDtypeStruct((rows, n), BF16),
        grid=(rows // tm, n // tn),
        in_specs=[pl.BlockSpec((tm, d), lambda i, j: (i, 0)),
                  pl.BlockSpec((1, d), lambda i, j: (0, 0)),
                  pl.BlockSpec((None, d, tn), lambda i, j: (layer, 0, j))],
        out_specs=pl.BlockSpec((tm, tn), lambda i, j: (i, j)),
        scratch_shapes=[pltpu.VMEM((tm, d), BF16)],
        compiler_params=_cparams("parallel", "arbitrary"),
        name="norm_proj",
    )(x, gain, w)


def _ret_chunk_kernel(q_ref, k_ref, v_ref, gr_ref, cos_ref, sin_ref, dmask_ref, s0_ref, og_ref, sout_ref, s_scr,
                      *, heads, dk, dv, n_pad):
    c = pl.program_id(1)
    n_chunks = pl.num_programs(1)
    half = dk // 2

    @pl.when(c == 0)
    def _():
        s_scr[...] = s0_ref[...]

    chunk = q_ref.shape[0]
    cos = cos_ref[...]
    sin = sin_ref[...]
    row = lax.broadcasted_iota(jnp.int32, (chunk, half), 0)
    rowf = row.astype(F32)
    cos_k = cos * (dk ** -0.5)
    sin_k = sin * (dk ** -0.5)

    def rope(x, cs, sn):
        x1 = x[:, :half]
        x2 = x[:, half:]
        return x1 * cs - x2 * sn, x2 * cs + x1 * sn

    hr = range(heads)
    lgs = [math.log(1.0 - 2.0 ** (-5.0 - h)) for h in hr]
    vsl = [slice(h * dv, (h + 1) * dv) for h in hr]
    qb, kb, qd, kd, vhs, dmask = [], [], [], [], [], []
    for h in hr:
        lg = lgs[h]
        qs = slice(h * dk, (h + 1) * dk)
        q1, q2 = rope(q_ref[:, qs].astype(F32), cos, sin)
        k1, k2 = rope(k_ref[:, qs].astype(F32), cos_k, sin_k)
        vh = v_ref[:, vsl[h]]
        if n_pad:
            k1 = jnp.where(row >= n_pad, k1, 0.0)
            k2 = jnp.where(row >= n_pad, k2, 0.0)
            vrow = lax.broadcasted_iota(jnp.int32, (chunk, dv), 0)
            vh = jnp.where(vrow >= n_pad, vh, jnp.zeros_like(vh))
        q_decay = jnp.exp((rowf + 1.0) * lg)
        k_decay = jnp.exp((chunk - 1.0 - rowf) * lg)
        dmask.append(dmask_ref[h])
        qb.append(_bf(jnp.concatenate([q1, q2], axis=1)))
        kb.append(_bf(jnp.concatenate([k1, k2], axis=1)))
        qd.append(_bf(jnp.concatenate([q1 * q_decay, q2 * q_decay], axis=1)))
        kd.append(_bf(jnp.concatenate([k1 * k_decay, k2 * k_decay], axis=1)))
        vhs.append(vh)
    s_old = [s_scr[h] for h in hr]
    scores = [_bf(_dot_nt(qb[h], kb[h]) * dmask[h]) for h in hr]
    cross = [_dot(qd[h], _bf(s_old[h])) for h in hr]
    inner = [_dot(scores[h], vhs[h]) for h in hr]
    for h in hr:
        o = inner[h] + cross[h]
        o = o * lax.rsqrt(jnp.mean(o * o, axis=-1, keepdims=True) + NORM_EPS)
        g = gr_ref[:, vsl[h]].astype(F32)
        og_ref[:, vsl[h]] = _bf(o * (g * _sigmoid(g)))
    for h in hr:
        s_scr[h] = math.exp(chunk * lgs[h]) * s_old[h] + _dot_tn(kd[h], vhs[h])

    @pl.when(c == n_chunks - 1)
    def _():
        sout_ref[0] = s_scr[...]


def _decay_mask(chunk, heads):
    rel = (jnp.arange(chunk)[:, None] - jnp.arange(chunk)[None, :]).astype(F32)
    lg = jnp.asarray([math.log(1.0 - 2.0 ** (-5.0 - h)) for h in range(heads)], F32)[:, None, None]
    return jnp.where(rel[None] >= 0, jnp.exp(lg * jnp.maximum(rel, 0.0)[None]), 0.0)


def _ret_chunks(proj, cos, sin, s0, *, chunk, n_seq, n_chunks, heads, dk, dv, n_pad):
    qk = heads * dk
    vw = heads * dv
    assert vw == 2 * qk

    def rowmap(col):
        return lambda b, c: (b * n_chunks + c, col)

    kern = functools.partial(_ret_chunk_kernel, heads=heads, dk=dk, dv=dv, n_pad=n_pad)
    return pl.pallas_call(
        kern,
        out_shape=(jax.ShapeDtypeStruct((n_seq * n_chunks * chunk, vw), BF16),
                   jax.ShapeDtypeStruct((n_seq, heads, dk, dv), F32)),
        grid=(n_seq, n_chunks),
        in_specs=[pl.BlockSpec((chunk, qk), rowmap(0)),
                  pl.BlockSpec((chunk, qk), rowmap(1)),
                  pl.BlockSpec((chunk, vw), rowmap(1)),
                  pl.BlockSpec((chunk, vw), rowmap(2)),
                  pl.BlockSpec((chunk, dk // 2), lambda b, c: (c, 0)),
                  pl.BlockSpec((chunk, dk // 2), lambda b, c: (c, 0)),
                  pl.BlockSpec((heads, chunk, chunk), lambda b, c: (0, 0, 0)),
                  pl.BlockSpec((heads, dk, dv), lambda b, c: (0, 0, 0))],
        out_specs=(pl.BlockSpec((chunk, vw), lambda b, c: (b * n_chunks + c, 0)),
                   pl.BlockSpec((1, heads, dk, dv), lambda b, c: (b, 0, 0, 0))),
        scratch_shapes=[pltpu.VMEM((heads, dk, dv), F32)],
        compiler_params=_cparams("parallel", "arbitrary"),
        name="ret_chunks",
    )(proj, proj, proj, proj, cos, sin, _decay_mask(chunk, heads), s0)


def _ret_step_kernel(q_ref, k_ref, v_ref, gr_ref, cos_ref, sin_ref, s_ref, *rest, heads, dk, dv, nb):
    og_ref, sout_ref = rest[-2:]
    half = dk // 2
    cos = cos_ref[...]
    sin = sin_ref[...]
    base = pl.program_id(0) * nb
    pad_rows = 16
    lane_w = 128
    sel_r = lax.broadcasted_iota(jnp.int32, (pad_rows, 2 * lane_w), 0)
    sel_c = lax.broadcasted_iota(jnp.int32, (pad_rows, 2 * lane_w), 1)
    spread = jnp.where((sel_r == 0) & (sel_c < lane_w) | (sel_r == 1) & (sel_c >= lane_w), 1.0, 0.0).astype(BF16)
    row_id = lax.broadcasted_iota(jnp.int32, (pad_rows, 1), 0)

    def rope(x):
        x1 = x[:, :half]
        x2 = x[:, half:]
        return jnp.concatenate([x1 * cos - x2 * sin, x2 * cos + x1 * sin], axis=1)

    for i in range(nb):
        n = base + i
        q_row, k_row, v_row, g_row = (ref[pl.ds(n, 1), :] for ref in (q_ref, k_ref, v_ref, gr_ref))
        o_parts = []
        for h in range(heads):
            gamma = 1.0 - 2.0 ** (-5.0 - h)
            qs = slice(h * dk, (h + 1) * dk)
            vs = slice(h * dv, (h + 1) * dv)
            q = rope(q_row[:, qs])
            k = rope(k_row[:, qs]) * (dk ** -0.5)
            v = v_row[:, vs]
            kq_rows = jnp.where(row_id == 0, jnp.broadcast_to(k, (pad_rows, dk)),
                                jnp.where(row_id == 1, jnp.broadcast_to(q, (pad_rows, dk)), 0.0))
            cols = _dot_tn(_bf(kq_rows), spread)
            k_col, q_col = cols[:, :lane_w], cols[:, lane_w:]
            o_tiles = []
            for t in range(dv // lane_w):
                ts = slice(t * lane_w, (t + 1) * lane_w)
                s_new = gamma * s_ref[0, i, h, :, ts] + k_col * v[:, ts]
                sout_ref[0, i, h, :, ts] = s_new
                o_tiles.append(jnp.sum(q_col * s_new, axis=0, keepdims=True))
            o = jnp.concatenate(o_tiles, axis=1)
            o = o * lax.rsqrt(jnp.mean(o * o, axis=-1, keepdims=True) + NORM_EPS)
            g = g_row[:, vs]
            o_parts.append(o * (g * _sigmoid(g)))
        og_ref[pl.ds(n, 1), :] = jnp.concatenate(o_parts, axis=1)


def _stacked_alias(stacked_out, n_inputs):
    if stacked_out is None:
        return [], [], {}
    return [stacked_out], [pl.BlockSpec(memory_space=pl.ANY)], {n_inputs: 1}


def _ret_step(q, k, v, gr, cos, sin, states, stacked_out, layer, *, heads, dk, dv, nb):
    n_seq = q.shape[0]
    full = lambda a: pl.BlockSpec(a.shape, lambda i: (0,) * a.ndim)
    kern = functools.partial(_ret_step_kernel, heads=heads, dk=dk, dv=dv, nb=nb)
    st_spec = pl.BlockSpec((1, nb, heads, dk, dv), lambda i: (layer, i, 0, 0, 0))
    extra, extra_specs, aliases = _stacked_alias(stacked_out, 7)
    return pl.pallas_call(
        kern,
        out_shape=(jax.ShapeDtypeStruct((n_seq, heads * dv), F32),
                   jax.ShapeDtypeStruct(states.shape, F32)),
        grid=(n_seq // nb,),
        in_specs=[full(q), full(k), full(v), full(gr), full(cos), full(sin), st_spec] + extra_specs,
        out_specs=(pl.BlockSpec((n_seq, heads * dv), lambda i: (0, 0)), st_spec),
        input_output_aliases=aliases,
        compiler_params=_cparams("arbitrary"),
        name="ret_step",
    )(q, k, v, gr, cos, sin, states, *extra)


def _proj_prep_kernel(*refs, layer0, small, tm, rc, hd, lw_, la_, seq_tiles, n_pad, n_head):
    (x_ref, gain_ref, wb_ref, lead_ref, mu_ref, w0_ref, w2_ref, a0_ref, a2_ref, g2_ref, kkp_ref, kap_ref,
     v0_ref, v1_ref, v2_ref, vf_ref, r_o, lw_o, k_o, v_o, kk_o, b_o, g_o, rw_o) = refs[:24]
    rw = _dot(_bf(_rms(x_ref[...], gain_ref[...])), wb_ref[...])
    rolled = pltpu.roll(rw, 1, 0)
    if small:
        row = lax.broadcasted_iota(jnp.int32, (tm, 1), 0)
        prev = jnp.where((row <= n_pad) | (row >= n_head), lead_ref[...], rolled)
        rw_o[...] = rw
    else:
        carry_ref = refs[24]
        is_start = (pl.program_id(0) % seq_tiles) == 0
        first = jnp.where(is_start, lead_ref[...], carry_ref[...])
        row0 = lax.broadcasted_iota(jnp.int32, (8, 1), 0) == 0
        prev = jnp.concatenate([jnp.where(row0, first, rolled[:8]), rolled[8:]], axis=0)
        carry_ref[...] = rw[tm - 1:tm]
        rw_o[0] = rw[tm - 1:tm]
    z = rw + (prev - rw) * mu_ref[...]

    z_l = z[:, 3 * rc:]
    wd = z_l[:, :lw_]
    ad = z_l[:, lw_:lw_ + la_]
    gd = z_l[:, lw_ + la_:]
    w_in = w0_ref[...] + _dot(_bf(jnp.tanh(wd)), w2_ref[...])
    lw_o[...] = (-math.exp(-0.5)) * _sigmoid(w_in)
    a = _sigmoid(a0_ref[...] + _dot(_bf(ad), a2_ref[...]))
    g_o[...] = _bf(_dot(_bf(_sigmoid(gd)), g2_ref[...]))

    r_o[...] = _bf(z[:, :rc])

    z_k = z[:, rc:2 * rc]
    kk = z_k * kkp_ref[...]
    ones = _group_ones(hd)
    kk = kk * lax.rsqrt(jnp.maximum(_group_sum(kk * kk, ones, split=False), KK_EPS * KK_EPS))
    kk_o[...] = _bf(kk)
    b_o[...] = _bf(kk * a)
    k_o[...] = _bf(z_k * (1.0 + (a - 1.0) * kap_ref[...]))

    z_v = z[:, 2 * rc:3 * rc]
    if layer0:
        v_o[...] = _bf(z_v)
    else:
        lora = _dot(_bf(_dot(_bf(z_v), v1_ref[...])), v2_ref[...])
        v_o[...] = _bf(z_v + (vf_ref[...].astype(F32) - z_v) * _sigmoid(v0_ref[...] + lora))


def _proj_prep(x, gain, wb, layer, lead, p, v_first, *, layer0, small, tm, seq_rows, hd, n_pad=0, n_head=0):
    rows, d = x.shape
    width = wb.shape[2]
    rc = p["w0"].shape[1]
    lw_ = p["w2"].shape[0]
    la_ = p["a2"].shape[0]
    n_tiles = rows // tm
    assert (n_tiles == 1) if small else (tm % 8 == 0 and seq_rows % tm == 0)
    tile = lambda w: pl.BlockSpec((tm, w), lambda i: (i, 0))
    const = lambda a: pl.BlockSpec(a.shape, lambda i: (0,) * a.ndim)
    params = [p["mu"], p["w0"], p["w2"], p["a0"], p["a2"], p["g2"], p["kk"], p["ka"], p["v0"], p["v1"], p["v2"]]
    kern = functools.partial(_proj_prep_kernel, layer0=layer0, small=small, tm=tm, rc=rc, hd=hd, lw_=lw_,
                             la_=la_, seq_tiles=max(seq_rows // tm, 1), n_pad=n_pad, n_head=n_head)
    out_bf = jax.ShapeDtypeStruct((rows, rc), BF16)
    if small:
        rw_shape, rw_spec, scratch = (rows, width), tile(width), []
    else:
        rw_shape, rw_spec = (n_tiles, 1, width), pl.BlockSpec((1, 1, width), lambda i: (i, 0, 0))
        scratch = [pltpu.VMEM((1, width), F32)]
    return pl.pallas_call(
        kern,
        out_shape=(out_bf, jax.ShapeDtypeStruct((rows, rc), F32), out_bf, out_bf, out_bf, out_bf, out_bf,
                   jax.ShapeDtypeStruct(rw_shape, F32)),
        grid=(n_tiles,),
        in_specs=[tile(d), const(gain),
                  pl.BlockSpec((None, d, width), lambda i: (layer, 0, 0), pipeline_mode=pl.Buffered(1)),
                  tile(width) if small else const(lead)] + [const(a) for a in params] + [tile(rc)],
        out_specs=tuple(tile(rc) for _ in range(7)) + (rw_spec,),
        scratch_shapes=scratch,
        compiler_params=_cparams("arbitrary"),
        name="proj_prep",
    )(x, gain, wb, lead, *params, v_first)


def _wkv_chunk_kernel(r_ref, lw_ref, k_ref, v_ref, kk_ref, b_ref, s0_ref, y_ref, sout_ref, s_scr,
                      *, heads, hd, n_pad, n_par):
    c = pl.program_id(1)
    n_chunks = pl.num_programs(1)
    C = CHUNK

    @pl.when(c == 0)
    def _():
        for s in range(n_par):
            s_scr[s] = s0_ref[...]

    tril = jnp.where(lax.broadcasted_iota(jnp.int32, (C, C), 0) >= lax.broadcasted_iota(jnp.int32, (C, C), 1),
                     1.0, 0.0).astype(BF16)
    ti2 = lax.broadcasted_iota(jnp.int32, (C, 2 * C), 0)
    tj2 = lax.broadcasted_iota(jnp.int32, (C, 2 * C), 1)
    right = tj2 >= C
    strict2 = ti2 > (tj2 & (C - 1))
    lower2 = ti2 >= (tj2 & (C - 1))

    def scaled(s):
        lw = lw_ref[s]
        hi = _bf(lw)
        r1 = lw - hi.astype(F32)
        mid = _bf(r1)
        lo = _bf(r1 - mid.astype(F32))
        cum = _dot(tril, hi) + _dot(tril, mid) + _dot(tril, lo)
        total = cum[C - 1:C, :]
        e_inv = jnp.exp(-cum)
        e_tail = jnp.exp(total - cum)
        kk = kk_ref[s].astype(F32)
        bb = b_ref[s].astype(F32)
        kx = k_ref[s].astype(F32)
        vx = v_ref[s]
        if n_pad:
            rowm = lax.broadcasted_iota(jnp.int32, kk.shape, 0) >= n_pad
            kk = jnp.where(rowm, kk, 0.0)
            bb = jnp.where(rowm, bb, 0.0)
            kx = jnp.where(rowm, kx, 0.0)
            vx = jnp.where(rowm, vx, jnp.zeros_like(vx))
        return dict(a=_bf(-kk * jnp.exp(cum - lw)),
                    b=_bf(bb * e_inv), k=_bf(kx * e_inv), r=_bf(r_ref[s].astype(F32) * jnp.exp(cum)),
                    b_hat=_bf(bb * e_tail), k_hat=_bf(kx * e_tail), v=vx, p_end=jnp.exp(total))

    seqs = [scaled(s) for s in range(n_par)]
    pairs = [(s, h) for s in range(n_par) for h in range(heads)]
    col = lambda s, h, name: seqs[s][name][:, h * hd:(h + 1) * hd]
    s0 = [s_scr[s, h] for s, h in pairs]
    s0b = [_bf(v) for v in s0]
    ar = [jnp.concatenate([col(s, h, "a"), col(s, h, "r")], axis=0) for s, h in pairs]
    s4 = [_dot_nt(ar[i], jnp.concatenate([col(s, h, "b"), col(s, h, "k")], axis=0))
          for i, (s, h) in enumerate(pairs)]
    from_state = [_dot_nt(ar[i], s0b[i]) for i in range(len(pairs))]
    top = [jnp.where(strict2, v[:C], 0.0) for v in s4]
    bot = [_bf(jnp.where(lower2, v[C:], 0.0)) for v in s4]
    x0 = [from_state[i][:C] + _dot(_bf(top[i][:, C:]), col(s, h, "v")) for i, (s, h) in enumerate(pairs)]
    z = [jnp.concatenate([top[i][:, :C], x0[i]], axis=1) for i in range(len(pairs))]
    for _ in range(6):
        zb = [_bf(v) for v in z]
        z = [_dot(zb[i][:, :C], zb[i]) + jnp.where(right, z[i], 0.0) for i in range(len(pairs))]
    uv = [jnp.concatenate([_bf(z[i][:, C:]), col(s, h, "v")], axis=0) for i, (s, h) in enumerate(pairs)]
    for i, (s, h) in enumerate(pairs):
        y_ref[s, :, h * hd:(h + 1) * hd] = from_state[i][C:] + _dot(bot[i], uv[i])
    for i, (s, h) in enumerate(pairs):
        s_scr[s, h] = s0[i] * seqs[s]["p_end"][:, h * hd:(h + 1) * hd] + _dot_tn(
            uv[i], jnp.concatenate([col(s, h, "b_hat"), col(s, h, "k_hat")], axis=0))

    @pl.when(c == n_chunks - 1)
    def _():
        sout_ref[...] = s_scr[...]


def _wkv_chunks(r, lw, k, v, kk, b, s0, *, n_par, heads, hd, n_pad):
    n_seq, tokens, rc = r.shape
    n_chunks = tokens // CHUNK
    tile = pl.BlockSpec((n_par, CHUNK, rc), lambda i, c: (i, c, 0))
    state = pl.BlockSpec((n_par, heads, hd, hd), lambda i, c: (i, 0, 0, 0))
    kern = functools.partial(_wkv_chunk_kernel, heads=heads, hd=hd, n_pad=n_pad, n_par=n_par)
    return pl.pallas_call(
        kern,
        out_shape=(jax.ShapeDtypeStruct((n_seq, tokens, rc), F32),
                   jax.ShapeDtypeStruct((n_seq, heads, hd, hd), F32)),
        grid=(n_seq // n_par, n_chunks),
        in_specs=[tile] * 6 + [pl.BlockSpec((heads, hd, hd), lambda i, c: (0, 0, 0))],
        out_specs=(tile, state),
        scratch_shapes=[pltpu.VMEM((n_par, heads, hd, hd), F32)],
        compiler_params=_cparams("parallel", "arbitrary"),
        name="wkv_chunks",
    )(r, lw, k, v, kk, b, s0)


def _wkv_step_kernel(r_ref, lw_ref, k_ref, v_ref, kk_ref, b_ref, s_ref, *rest, hd, slab):
    y_ref, sout_ref = rest[-2:]
    w = jnp.exp(lw_ref[...])
    kk, bb, kx, rx, vx = kk_ref[...], b_ref[...], k_ref[...], r_ref[...], v_ref[...]
    per_slab = slab // hd
    ys = []
    for j in range(hd // per_slab):
        cols = slice(j * slab, (j + 1) * slab)
        st = s_ref[0, :, cols].T
        outs = []
        for t in range(per_slab):
            vi = j * per_slab + t
            s = st[t * hd:(t + 1) * hd, :]
            s_kk = jnp.sum(s * kk, axis=0, keepdims=True)
            s_new = s * w - s_kk * bb + vx[vi:vi + 1, :] * kx
            ys.append(jnp.sum(s_new * rx, axis=0, keepdims=True))
            outs.append(s_new)
        sout_ref[0, :, cols] = jnp.concatenate(outs, axis=0).T
    y_ref[...] = jnp.concatenate(ys, axis=0)


def _wkv_step(r, lw, k, v, kk, b, states, stacked_out, layer, *, heads, hd):
    rc, n_seq = r.shape
    slab = max(hd, 128)
    vec = pl.BlockSpec((hd, n_seq), lambda h: (h, 0))
    st_spec = pl.BlockSpec((1, n_seq, hd * hd), lambda h: (layer, 0, h))
    kern = functools.partial(_wkv_step_kernel, hd=hd, slab=slab)
    extra, extra_specs, aliases = _stacked_alias(stacked_out, 7)
    return pl.pallas_call(
        kern,
        out_shape=(jax.ShapeDtypeStruct((rc, n_seq), F32), jax.ShapeDtypeStruct(states.shape, F32)),
        grid=(heads,),
        in_specs=[vec] * 6 + [st_spec] + extra_specs,
        out_specs=(vec, st_spec),
        input_output_aliases=aliases,
        compiler_params=_cparams("parallel"),
        name="wkv_step",
    )(r, lw, k, v, kk, b, states, *extra)


def _mix_kernel(og_ref, y_ref, r_ref, k_ref, v_ref, g_ref, gates_ref, x_ref,
                wret_ref, wrw_ref, wout_ref, lnw_ref, lnb_ref, rk_ref, gain_ref, o_ref, *, hd, d):
    y_ret = _dot(og_ref[...], wret_ref[...])
    ones = _group_ones(hd)
    y = y_ref[...]
    inv_n = 1.0 / hd
    mean = _group_sum(y, ones, split=False) * inv_n
    dlt = y - mean
    var = _group_sum(dlt * dlt, ones, split=False) * inv_n
    yn = dlt * lax.rsqrt(var + LNX_EPS) * lnw_ref[...] + lnb_ref[...]
    rkk = r_ref[...].astype(F32) * k_ref[...].astype(F32) * rk_ref[...]
    bonus = _group_sum(rkk, ones, split=False) * v_ref[...].astype(F32)
    y_rw = _dot(_bf((yn + bonus) * g_ref[...].astype(F32)), wrw_ref[...])
    gates = gates_ref[...].astype(F32)
    mix = _sigmoid(gates[:, :d]) * y_ret + _sigmoid(gates[:, d:]) * y_rw
    o_ref[...] = x_ref[...] + _rms(_dot(_bf(mix), wout_ref[...]), gain_ref[...])


def _mix(og, y, r, k, v, g, proj, x, w_ret, w_rw, w_out, lnw, lnb, rk, gain, *, layer, tm, hd, gate_block):
    rows, d = x.shape
    tile = lambda a: pl.BlockSpec((tm, a.shape[1]), lambda i: (i, 0))
    const = lambda a: pl.BlockSpec(a.shape, lambda i: (0,) * a.ndim)
    stacked = lambda a: pl.BlockSpec((None,) + a.shape[1:], lambda i: (layer, 0, 0),
                                     pipeline_mode=pl.Buffered(1))
    kern = functools.partial(_mix_kernel, hd=hd, d=d)
    return pl.pallas_call(
        kern,
        out_shape=jax.ShapeDtypeStruct((rows, d), F32),
        grid=(rows // tm,),
        in_specs=[tile(og), tile(y), tile(r), tile(k), tile(v), tile(g),
                  pl.BlockSpec((tm, 2 * d), lambda i: (i, gate_block)), tile(x),
                  stacked(w_ret), stacked(w_rw), stacked(w_out), const(lnw), const(lnb), const(rk), const(gain)],
        out_specs=pl.BlockSpec((tm, d), lambda i: (i, 0)),
        compiler_params=_cparams("parallel"),
        name="mix",
    )(og, y, r, k, v, g, proj, x, w_ret, w_rw, w_out, lnw, lnb, rk, gain)


def _ffn_kernel(x_ref, g2_ref, g3_ref, w1_ref, w3_ref, w2_ref, o_ref, hn_ref, acc_ref):
    j = pl.program_id(1)

    @pl.when(j == 0)
    def _():
        hn_ref[...] = _bf(_rms(x_ref[...], g2_ref[...]))
        acc_ref[...] = jnp.zeros_like(acc_ref)

    tm = hn_ref.shape[0]
    sub = FFN_SUB_ROWS if tm % FFN_SUB_ROWS == 0 else tm
    blocks = [slice(r, r + sub) for r in range(0, tm, sub)]

    def up(rows):
        hn = hn_ref[rows, :]
        return _dot(hn, w1_ref[...]), _dot(hn, w3_ref[...])

    def down(rows, ab):
        a, b = ab
        acc_ref[rows, :] += _dot(_bf((a * _sigmoid(a)) * b), w2_ref[...])

    pending = up(blocks[0])
    for r in range(1, len(blocks)):
        nxt = up(blocks[r])
        down(blocks[r - 1], pending)
        pending = nxt
    down(blocks[-1], pending)

    @pl.when(j == pl.num_programs(1) - 1)
    def _():
        o_ref[...] = x_ref[...] + _rms(acc_ref[...], g3_ref[...])


def _ffn(x, gain2, gain3, w1, w3, w2, *, layer, tm, tf):
    rows, d = x.shape
    f = w1.shape[2]
    mode = dict(pipeline_mode=pl.Buffered(1)) if tf == f else {}
    return pl.pallas_call(
        _ffn_kernel,
        out_shape=jax.ShapeDtypeStruct((rows, d), F32),
        grid=(rows // tm, f // tf),
        in_specs=[pl.BlockSpec((tm, d), lambda i, j: (i, 0)),
                  pl.BlockSpec((1, d), lambda i, j: (0, 0)),
                  pl.BlockSpec((1, d), lambda i, j: (0, 0)),
                  pl.BlockSpec((None, d, tf), lambda i, j: (layer, 0, j), **mode),
                  pl.BlockSpec((None, d, tf), lambda i, j: (layer, 0, j), **mode),
                  pl.BlockSpec((None, tf, d), lambda i, j: (layer, j, 0), **mode)],
        out_specs=pl.BlockSpec((tm, d), lambda i, j: (i, 0)),
        scratch_shapes=[pltpu.VMEM((tm, d), BF16), pltpu.VMEM((tm, d), F32)],
        compiler_params=_cparams("parallel", "arbitrary"),
        name="ffn",
    )(x, gain2, gain3, w1, w3, w2)


def _pick(n, prefs):
    for t in prefs:
        if n % t == 0:
            return t
    return n


def _rope_tables(pos, half):
    inv_freq = 1.0 / (ROPE_BASE ** (jnp.arange(half, dtype=F32) / half))
    ang = pos.astype(F32)[:, None] * inv_freq[None, :]
    return jnp.cos(ang), jnp.sin(ang)


def kernel(x_prompt, x_sample, state_ret, state_wkv, state_shift, meta_tokens, norm_gain, w_in, w_ret_out,
           w_rwkv_out, w_out, rwkv_mu, rwkv_w0, rwkv_w2, rwkv_a0, rwkv_a2, rwkv_g2, rwkv_kk, rwkv_ka, rwkv_rk,
           rwkv_lnx_w, rwkv_lnx_b, rwkv_v0, rwkv_v1, rwkv_v2, ffn_w1, ffn_w3, ffn_w2):
    bp, sp, d = x_prompt.shape
    ns = x_sample.shape[0]
    depth = w_in.shape[0]
    n_meta = meta_tokens.shape[0]
    _, _, rh, dk, dv = state_ret.shape
    _, _, wh, hd, _ = state_wkv.shape
    qk, vw, rc = rh * dk, rh * dv, wh * hd
    rw_w = state_shift.shape[-1]
    assert x_sample.shape[1] == 1 and n_meta <= CHUNK and sp % CHUNK == 0
    assert vw == 2 * qk and d == qk and rc == d and hd & (hd - 1) == 0 and MXU_TILE % hd == 0
    n_chunks = sp // CHUNK
    n_pad = CHUNK - n_meta
    main_rows = bp * sp
    small_rows = CHUNK + ns
    half = dk // 2

    xm = x_prompt.reshape(main_rows, d)
    xs = jnp.concatenate([jnp.zeros((n_pad, d), F32), meta_tokens.astype(F32), x_sample.reshape(ns, d)], axis=0)

    cos_m, sin_m = _rope_tables(n_meta + jnp.arange(sp), half)
    cos_t, sin_t = _rope_tables(jnp.arange(CHUNK) - n_pad, half)
    cos_s, sin_s = _rope_tables(jnp.full((1,), PAST_LEN), half)

    split = 2 * qk + 2 * vw
    wa = jnp.concatenate([_bf(w_in[:, :, :split]), _bf(w_in[:, :, split + rw_w:])], axis=-1)
    wb = _bf(w_in[:, :, split:split + rw_w])
    gate_block = split // (2 * d)
    assert gate_block * 2 * d == split

    tm_m = _pick(main_rows, (1024, 512, 256, 128, 64))
    tm_mix = _pick(main_rows, (512, 256, 128, 64))
    ret_chunk = _pick(sp, (4 * CHUNK, 2 * CHUNK, CHUNK))
    tm_proj = _pick(main_rows, (1024, 512, 256, 128, 64))
    tn_a = _pick(wa.shape[-1], (2048, 1024, 512, 256))
    tf = ffn_w1.shape[-1]
    seq_tm = _pick(sp, PREP_TILES)

    w_ret_b, w_rw_b, w_out_b = _bf(w_ret_out), _bf(w_rwkv_out), _bf(w_out)
    w1_b, w3_b, w2_b = _bf(ffn_w1), _bf(ffn_w3), _bf(ffn_w2)
    ret_p, wkv_p, sh_p, sh_s = [], [], [], []
    ret_s = wkv_s = None
    wkv_flat = state_wkv.reshape(depth, ns, wh * hd * hd)
    vf_m = vf_s = None
    for l in range(depth):
        g = norm_gain[l]
        p = dict(mu=rwkv_mu[l][None], w0=rwkv_w0[l][None], w2=_bf(rwkv_w2[l]), a0=rwkv_a0[l][None],
                 a2=_bf(rwkv_a2[l]), g2=_bf(rwkv_g2[l]), kk=rwkv_kk[l][None], ka=rwkv_ka[l][None])
        lv = max(l - 1, 0)
        p.update(v0=rwkv_v0[lv][None], v1=_bf(rwkv_v1[lv]), v2=_bf(rwkv_v2[lv]))

        pa_s = _norm_proj(xs, g[0][None], wa, l, small_rows, tn_a)
        prev_s = jnp.concatenate([jnp.zeros((CHUNK, rw_w), F32), state_shift[l]], axis=0)
        if l == 0:
            vf_s = jnp.zeros((small_rows, rc), BF16)
        r_s, lw_s, k_s, v_s, kk_s, b_s, g_s, pb_sf = _proj_prep(
            xs, g[0][None], wb, l, prev_s, p, vf_s, layer0=(l == 0), small=True, tm=small_rows,
            seq_rows=small_rows, hd=hd, n_pad=n_pad, n_head=CHUNK)
        if l == 0:
            vf_s = v_s

        og_meta, s_ret_meta = _ret_chunks(pa_s, cos_t, sin_t, jnp.zeros((rh, dk, dv), F32), chunk=CHUNK,
                                          n_seq=1, n_chunks=1, heads=rh, dk=dk, dv=dv, n_pad=n_pad)
        meta_in = [a[:CHUNK].reshape(1, CHUNK, rc) for a in (r_s, lw_s, k_s, v_s, kk_s, b_s)]
        y_meta, s_wkv_meta = _wkv_chunks(*meta_in, jnp.zeros((wh, hd, hd), F32), n_par=1, heads=wh, hd=hd,
                                         n_pad=n_pad)
        y_meta = y_meta.reshape(CHUNK, rc)
        pa_sf = pa_s[CHUNK:].astype(F32)
        og_smp, ret_s = _ret_step(pa_sf[:, :qk], pa_sf[:, qk:2 * qk], pa_sf[:, 2 * qk:2 * qk + vw],
                                  pa_sf[:, 2 * qk + vw:split], cos_s, sin_s, state_ret, ret_s, l,
                                  heads=rh, dk=dk, dv=dv, nb=_pick(ns, (2, 1)))
        smp = [a[CHUNK:].astype(F32).T for a in (r_s, lw_s, k_s, v_s, kk_s, b_s)]
        y_smp, wkv_s = _wkv_step(*smp, wkv_flat, wkv_s, l, heads=wh, hd=hd)
        og_s = jnp.concatenate([og_meta, _bf(og_smp)], axis=0)
        y_s = jnp.concatenate([y_meta, y_smp.T], axis=0)
        sh_s.append(pb_sf[CHUNK:])

        pa_m = _norm_proj(xm, g[0][None], wa, l, tm_proj, tn_a)
        if l == 0:
            vf_m = jnp.zeros((main_rows, rc), BF16)
        r_m, lw_m, k_m, v_m, kk_m, b_m, g_m, tails = _proj_prep(
            xm, g[0][None], wb, l, pb_sf[CHUNK - 1:CHUNK], p, vf_m, layer0=(l == 0), small=False, tm=seq_tm,
            seq_rows=sp, hd=hd)
        if l == 0:
            vf_m = v_m
        og_m, s_ret_m = _ret_chunks(pa_m, cos_m, sin_m, s_ret_meta[0], chunk=ret_chunk, n_seq=bp,
                                    n_chunks=sp // ret_chunk, heads=rh, dk=dk, dv=dv, n_pad=0)
        main_in = [a.reshape(bp, sp, rc) for a in (r_m, lw_m, k_m, v_m, kk_m, b_m)]
        y_m, s_wkv_m = _wkv_chunks(*main_in, s_wkv_meta[0], n_par=_pick(bp, (WKV_PAR, 1)), heads=wh, hd=hd,
                                   n_pad=0)
        y_m = y_m.reshape(main_rows, rc)
        ret_p.append(s_ret_m)
        wkv_p.append(s_wkv_m)
        sh_p.append(tails.reshape(bp, sp // seq_tm, rw_w)[:, -1])

        mixw = (w_ret_b, w_rw_b, w_out_b, rwkv_lnx_w[l][None], rwkv_lnx_b[l][None], rwkv_rk[l][None], g[1][None])
        ffnw = (g[2][None], g[3][None], w1_b, w3_b, w2_b)
        mix_kw = dict(layer=l, hd=hd, gate_block=gate_block)
        xs = _mix(og_s, y_s, r_s, k_s, v_s, g_s, pa_s, xs, *mixw, tm=small_rows, **mix_kw)
        xs = _ffn(xs, *ffnw, layer=l, tm=small_rows, tf=tf)
        xm = _mix(og_m, y_m, r_m, k_m, v_m, g_m, pa_m, xm, *mixw, tm=tm_mix, **mix_kw)
        xm = _ffn(xm, *ffnw, layer=l, tm=tm_m, tf=tf)

    return (xm.reshape(bp, sp, d), xs[CHUNK:].reshape(ns, 1, d), jnp.stack(ret_p), jnp.stack(wkv_p),
            jnp.stack(sh_p), ret_s, wkv_s.reshape(state_wkv.shape), jnp.stack(sh_s))
```

```python
import functools
import math

import jax
import jax.numpy as jnp
from jax import lax
from jax.experimental import pallas as pl
from jax.experimental.pallas import tpu as pltpu

F32 = jnp.float32
BF16 = jnp.bfloat16

NORM_EPS = 1e-6
LNX_EPS = 64e-5
ROPE_BASE = 10000.0
PAST_LEN = 16384
KK_EPS = 1e-12

CHUNK = 64
MXU_TILE = 256
VMEM_LIMIT = 56 * 1024 * 1024
PREP_TILES = (512, 256, 128, 64)
FFN_SUB_ROWS = 256
WKV_PAR = 4


def _cparams(*sem):
    return pltpu.CompilerParams(dimension_semantics=sem, vmem_limit_bytes=VMEM_LIMIT)


def _dot(a, b):
    return jnp.dot(a, b, preferred_element_type=F32)


def _dot_nt(a, b):
    return lax.dot_general(a, b, (((1,), (1,)), ((), ())), preferred_element_type=F32)


def _dot_tn(a, b):
    return lax.dot_general(a, b, (((0,), (0,)), ((), ())), preferred_element_type=F32)


def _bf(x):
    return x.astype(BF16)


def _sigmoid(x):
    return 0.5 * jnp.tanh(0.5 * x) + 0.5


def _rms(x, gain):
    return x * lax.rsqrt(jnp.mean(x * x, axis=-1, keepdims=True) + NORM_EPS) * gain


def _group_ones(group):
    shift = group.bit_length() - 1
    r = lax.broadcasted_iota(jnp.int32, (MXU_TILE, MXU_TILE), 0) >> shift
    c = lax.broadcasted_iota(jnp.int32, (MXU_TILE, MXU_TILE), 1) >> shift
    return jnp.where(r == c, 1.0, 0.0).astype(BF16)


def _group_sum(x, ones, split=True):
    hi = _bf(x)
    lo = _bf(x - hi.astype(F32)) if split else None
    parts = []
    for j in range(x.shape[1] // MXU_TILE):
        sl = slice(j * MXU_TILE, (j + 1) * MXU_TILE)
        part = _dot(hi[:, sl], ones)
        parts.append(part + _dot(lo[:, sl], ones) if split else part)
    return parts[0] if len(parts) == 1 else jnp.concatenate(parts, axis=1)


def _norm_proj_kernel(x_ref, g_ref, w_ref, o_ref, xn_ref):
    @pl.when(pl.program_id(1) == 0)
    def _():
        xn_ref[...] = _bf(_rms(x_ref[...], g_ref[...]))

    o_ref[...] = _dot(xn_ref[...], w_ref[...]).astype(o_ref.dtype)


def _norm_proj(x, gain, w, layer, tm, tn):
    rows, d = x.shape
    n = w.shape[2]
    return pl.pallas_call(
        _norm_proj_kernel,
        out_shape=jax.ShapeDtypeStruct((rows, n), BF16),
        grid=(rows // tm, n // tn),
        in_specs=[pl.BlockSpec((tm, d), lambda i, j: (i, 0)),
                  pl.BlockSpec((1, d), lambda i, j: (0, 0)),
                  pl.BlockSpec((None, d, tn), lambda i, j: (layer, 0, j))],
        out_specs=pl.BlockSpec((tm, tn), lambda i, j: (i, j)),
        scratch_shapes=[pltpu.VMEM((tm, d), BF16)],
        compiler_params=_cparams("parallel", "arbitrary"),
        name="norm_proj",
    )(x, gain, w)


def _ret_chunk_kernel(q_ref, k_ref, v_ref, gr_ref, cos_ref, sin_ref, dmask_ref, s0_ref, og_ref, sout_ref, s_scr,
                      *, heads, dk, dv, n_pad):
    c = pl.program_id(1)
    n_chunks = pl.num_programs(1)
    half = dk // 2

    @pl.when(c == 0)
    def _():
        s_scr[...] = s0_ref[...]

    chunk = q_ref.shape[0]
    cos = cos_ref[...]
    sin = sin_ref[...]
    row = lax.broadcasted_iota(jnp.int32, (chunk, half), 0)
    rowf = row.astype(F32)
    cos_k = cos * (dk ** -0.5)
    sin_k = sin * (dk ** -0.5)

    def rope(x, cs, sn):
        x1 = x[:, :half]
        x2 = x[:, half:]
        return x1 * cs - x2 * sn, x2 * cs + x1 * sn

    hr = range(heads)
    lgs = [math.log(1.0 - 2.0 ** (-5.0 - h)) for h in hr]
    vsl = [slice(h * dv, (h + 1) * dv) for h in hr]
    qb, kb, qd, kd, vhs, dmask = [], [], [], [], [], []
    for h in hr:
        lg = lgs[h]
        qs = slice(h * dk, (h + 1) * dk)
        q1, q2 = rope(q_ref[:, qs].astype(F32), cos, sin)
        k1, k2 = rope(k_ref[:, qs].astype(F32), cos_k, sin_k)
        vh = v_ref[:, vsl[h]]
        if n_pad:
            k1 = jnp.where(row >= n_pad, k1, 0.0)
            k2 = jnp.where(row >= n_pad, k2, 0.0)
            vrow = lax.broadcasted_iota(jnp.int32, (chunk, dv), 0)
            vh = jnp.where(vrow >= n_pad, vh, jnp.zeros_like(vh))
        q_decay = jnp.exp((rowf + 1.0) * lg)
        k_decay = jnp.exp((chunk - 1.0 - rowf) * lg)
        dmask.append(dmask_ref[h])
        qb.append(_bf(jnp.concatenate([q1, q2], axis=1)))
        kb.append(_bf(jnp.concatenate([k1, k2], axis=1)))
        qd.append(_bf(jnp.concatenate([q1 * q_decay, q2 * q_decay], axis=1)))
        kd.append(_bf(jnp.concatenate([k1 * k_decay, k2 * k_decay], axis=1)))
        vhs.append(vh)
    s_old = [s_scr[h] for h in hr]
    scores = [_bf(_dot_nt(qb[h], kb[h]) * dmask[h]) for h in hr]
    cross = [_dot(qd[h], _bf(s_old[h])) for h in hr]
    inner = [_dot(scores[h], vhs[h]) for h in hr]
    for h in hr:
        o = inner[h] + cross[h]
        o = o * lax.rsqrt(jnp.mean(o * o, axis=-1, keepdims=True) + NORM_EPS)
        g = gr_ref[:, vsl[h]].astype(F32)
        og_ref[:, vsl[h]] = _bf(o * (g * _sigmoid(g)))
    for h in hr:
        s_scr[h] = math.exp(chunk * lgs[h]) * s_old[h] + _dot_tn(kd[h], vhs[h])

    @pl.when(c == n_chunks - 1)
    def _():
        sout_ref[0] = s_scr[...]


def _decay_mask(chunk, heads):
    rel = (jnp.arange(chunk)[:, None] - jnp.arange(chunk)[None, :]).astype(F32)
    lg = jnp.asarray([math.log(1.0 - 2.0 ** (-5.0 - h)) for h in range(heads)], F32)[:, None, None]
    return jnp.where(rel[None] >= 0, jnp.exp(lg * jnp.maximum(rel, 0.0)[None]), 0.0)


def _ret_chunks(proj, cos, sin, s0, *, chunk, n_seq, n_chunks, heads, dk, dv, n_pad):
    qk = heads * dk
    vw = heads * dv
    assert vw == 2 * qk

    def rowmap(col):
        return lambda b, c: (b * n_chunks + c, col)

    kern = functools.partial(_ret_chunk_kernel, heads=heads, dk=dk, dv=dv, n_pad=n_pad)
    return pl.pallas_call(
        kern,
        out_shape=(jax.ShapeDtypeStruct((n_seq * n_chunks * chunk, vw), BF16),
                   jax.ShapeDtypeStruct((n_seq, heads, dk, dv), F32)),
        grid=(n_seq, n_chunks),
        in_specs=[pl.BlockSpec((chunk, qk), rowmap(0)),
                  pl.BlockSpec((chunk, qk), rowmap(1)),
                  pl.BlockSpec((chunk, vw), rowmap(1)),
                  pl.BlockSpec((chunk, vw), rowmap(2)),
                  pl.BlockSpec((chunk, dk // 2), lambda b, c: (c, 0)),
                  pl.BlockSpec((chunk, dk // 2), lambda b, c: (c, 0)),
                  pl.BlockSpec((heads, chunk, chunk), lambda b, c: (0, 0, 0)),
                  pl.BlockSpec((heads, dk, dv), lambda b, c: (0, 0, 0))],
        out_specs=(pl.BlockSpec((chunk, vw), lambda b, c: (b * n_chunks + c, 0)),
                   pl.BlockSpec((1, heads, dk, dv), lambda b, c: (b, 0, 0, 0))),
        scratch_shapes=[pltpu.VMEM((heads, dk, dv), F32)],
        compiler_params=_cparams("parallel", "arbitrary"),
        name="ret_chunks",
    )(proj, proj, proj, proj, cos, sin, _decay_mask(chunk, heads), s0)


def _ret_step_kernel(q_ref, k_ref, v_ref, gr_ref, cos_ref, sin_ref, s_ref, *rest, heads, dk, dv, nb):
    og_ref, sout_ref = rest[-2:]
    half = dk // 2
    cos = cos_ref[...]
    sin = sin_ref[...]
    base = pl.program_id(0) * nb
    pad_rows = 16
    lane_w = 128
    sel_r = lax.broadcasted_iota(jnp.int32, (pad_rows, 2 * lane_w), 0)
    sel_c = lax.broadcasted_iota(jnp.int32, (pad_rows, 2 * lane_w), 1)
    spread = jnp.where((sel_r == 0) & (sel_c < lane_w) | (sel_r == 1) & (sel_c >= lane_w), 1.0, 0.0).astype(BF16)
    row_id = lax.broadcasted_iota(jnp.int32, (pad_rows, 1), 0)

    def rope(x):
        x1 = x[:, :half]
        x2 = x[:, half:]
        return jnp.concatenate([x1 * cos - x2 * sin, x2 * cos + x1 * sin], axis=1)

    for i in range(nb):
        n = base + i
        q_row, k_row, v_row, g_row = (ref[pl.ds(n, 1), :] for ref in (q_ref, k_ref, v_ref, gr_ref))
        o_parts = []
        for h in range(heads):
            gamma = 1.0 - 2.0 ** (-5.0 - h)
            qs = slice(h * dk, (h + 1) * dk)
            vs = slice(h * dv, (h + 1) * dv)
            q = rope(q_row[:, qs])
            k = rope(k_row[:, qs]) * (dk ** -0.5)
            v = v_row[:, vs]
            kq_rows = jnp.where(row_id == 0, jnp.broadcast_to(k, (pad_rows, dk)),
                                jnp.where(row_id == 1, jnp.broadcast_to(q, (pad_rows, dk)), 0.0))
            cols = _dot_tn(_bf(kq_rows), spread)
            k_col, q_col = cols[:, :lane_w], cols[:, lane_w:]
            o_tiles = []
            for t in range(dv // lane_w):
                ts = slice(t * lane_w, (t + 1) * lane_w)
                s_new = gamma * s_ref[0, i, h, :, ts] + k_col * v[:, ts]
                sout_ref[0, i, h, :, ts] = s_new
                o_tiles.append(jnp.sum(q_col * s_new, axis=0, keepdims=True))
            o = jnp.concatenate(o_tiles, axis=1)
            o = o * lax.rsqrt(jnp.mean(o * o, axis=-1, keepdims=True) + NORM_EPS)
            g = g_row[:, vs]
            o_parts.append(o * (g * _sigmoid(g)))
        og_ref[pl.ds(n, 1), :] = jnp.concatenate(o_parts, axis=1)


def _stacked_alias(stacked_out, n_inputs):
    if stacked_out is None:
        return [], [], {}
    return [stacked_out], [pl.BlockSpec(memory_space=pl.ANY)], {n_inputs: 1}


def _ret_step(q, k, v, gr, cos, sin, states, stacked_out, layer, *, heads, dk, dv, nb):
    n_seq = q.shape[0]
    full = lambda a: pl.BlockSpec(a.shape, lambda i: (0,) * a.ndim)
    kern = functools.partial(_ret_step_kernel, heads=heads, dk=dk, dv=dv, nb=nb)
    st_spec = pl.BlockSpec((1, nb, heads, dk, dv), lambda i: (layer, i, 0, 0, 0))
    extra, extra_specs, aliases = _stacked_alias(stacked_out, 7)
    return pl.pallas_call(
        kern,
        out_shape=(jax.ShapeDtypeStruct((n_seq, heads * dv), F32),
                   jax.ShapeDtypeStruct(states.shape, F32)),
        grid=(n_seq // nb,),
        in_specs=[full(q), full(k), full(v), full(gr), full(cos), full(sin), st_spec] + extra_specs,
        out_specs=(pl.BlockSpec((n_seq, heads * dv), lambda i: (0, 0)), st_spec),
        input_output_aliases=aliases,
        compiler_params=_cparams("arbitrary"),
        name="ret_step",
    )(q, k, v, gr, cos, sin, states, *extra)


def _proj_prep_kernel(*refs, layer0, small, tm, rc, hd, lw_, la_, seq_tiles, n_pad, n_head):
    (x_ref, gain_ref, wb_ref, lead_ref, mu_ref, w0_ref, w2_ref, a0_ref, a2_ref, g2_ref, kkp_ref, kap_ref,
     v0_ref, v1_ref, v2_ref, vf_ref, r_o, lw_o, k_o, v_o, kk_o, b_o, g_o, rw_o) = refs[:24]
    rw = _dot(_bf(_rms(x_ref[...], gain_ref[...])), wb_ref[...])
    rolled = pltpu.roll(rw, 1, 0)
    if small:
        row = lax.broadcasted_iota(jnp.int32, (tm, 1), 0)
        prev = jnp.where((row <= n_pad) | (row >= n_head), lead_ref[...], rolled)
        rw_o[...] = rw
    else:
        carry_ref = refs[24]
        is_start = (pl.program_id(0) % seq_tiles) == 0
        first = jnp.where(is_start, lead_ref[...], carry_ref[...])
        row0 = lax.broadcasted_iota(jnp.int32, (8, 1), 0) == 0
        prev = jnp.concatenate([jnp.where(row0, first, rolled[:8]), rolled[8:]], axis=0)
        carry_ref[...] = rw[tm - 1:tm]
        rw_o[0] = rw[tm - 1:tm]
    z = rw + (prev - rw) * mu_ref[...]

    z_l = z[:, 3 * rc:]
    wd = z_l[:, :lw_]
    ad = z_l[:, lw_:lw_ + la_]
    gd = z_l[:, lw_ + la_:]
    w_in = w0_ref[...] + _dot(_bf(jnp.tanh(wd)), w2_ref[...])
    lw_o[...] = (-math.exp(-0.5)) * _sigmoid(w_in)
    a = _sigmoid(a0_ref[...] + _dot(_bf(ad), a2_ref[...]))
    g_o[...] = _bf(_dot(_bf(_sigmoid(gd)), g2_ref[...]))

    r_o[...] = _bf(z[:, :rc])

    z_k = z[:, rc:2 * rc]
    kk = z_k * kkp_ref[...]
    ones = _group_ones(hd)
    kk = kk * lax.rsqrt(jnp.maximum(_group_sum(kk * kk, ones, split=False), KK_EPS * KK_EPS))
    kk_o[...] = _bf(kk)
    b_o[...] = _bf(kk * a)
    k_o[...] = _bf(z_k * (1.0 + (a - 1.0) * kap_ref[...]))

    z_v = z[:, 2 * rc:3 * rc]
    if layer0:
        v_o[...] = _bf(z_v)
    else:
        lora = _dot(_bf(_dot(_bf(z_v), v1_ref[...])), v2_ref[...])
        v_o[...] = _bf(z_v + (vf_ref[...].astype(F32) - z_v) * _sigmoid(v0_ref[...] + lora))


def _proj_prep(x, gain, wb, layer, lead, p, v_first, *, layer0, small, tm, seq_rows, hd, n_pad=0, n_head=0):
    rows, d = x.shape
    width = wb.shape[2]
    rc = p["w0"].shape[1]
    lw_ = p["w2"].shape[0]
    la_ = p["a2"].shape[0]
    n_tiles = rows // tm
    assert (n_tiles == 1) if small else (tm % 8 == 0 and seq_rows % tm == 0)
    tile = lambda w: pl.BlockSpec((tm, w), lambda i: (i, 0))
    const = lambda a: pl.BlockSpec(a.shape, lambda i: (0,) * a.ndim)
    params = [p["mu"], p["w0"], p["w2"], p["a0"], p["a2"], p["g2"], p["kk"], p["ka"], p["v0"], p["v1"], p["v2"]]
    kern = functools.partial(_proj_prep_kernel, layer0=layer0, small=small, tm=tm, rc=rc, hd=hd, lw_=lw_,
                             la_=la_, seq_tiles=max(seq_rows // tm, 1), n_pad=n_pad, n_head=n_head)
    out_bf = jax.ShapeDtypeStruct((rows, rc), BF16)
    if small:
        rw_shape, rw_spec, scratch = (rows, width), tile(width), []
    else:
        rw_shape, rw_spec = (n_tiles, 1, width), pl.BlockSpec((1, 1, width), lambda i: (i, 0, 0))
        scratch = [pltpu.VMEM((1, width), F32)]
    return pl.pallas_call(
        kern,
        out_shape=(out_bf, jax.ShapeDtypeStruct((rows, rc), F32), out_bf, out_bf, out_bf, out_bf, out_bf,
                   jax.ShapeDtypeStruct(rw_shape, F32)),
        grid=(n_tiles,),
        in_specs=[tile(d), const(gain),
                  pl.BlockSpec((None, d, width), lambda i: (layer, 0, 0), pipeline_mode=pl.Buffered(1)),
                  tile(width) if small else const(lead)] + [const(a) for a in params] + [tile(rc)],
        out_specs=tuple(tile(rc) for _ in range(7)) + (rw_spec,),
        scratch_shapes=scratch,
        compiler_params=_cparams("arbitrary"),
        name="proj_prep",
    )(x, gain, wb, lead, *params, v_first)


def _wkv_chunk_kernel(r_ref, lw_ref, k_ref, v_ref, kk_ref, b_ref, s0_ref, y_ref, sout_ref, s_scr,
                      *, heads, hd, n_pad, n_par):
    c = pl.program_id(1)
    n_chunks = pl.num_programs(1)
    C = CHUNK

    @pl.when(c == 0)
    def _():
        for s in range(n_par):
            s_scr[s] = s0_ref[...]

    tril = jnp.where(lax.broadcasted_iota(jnp.int32, (C, C), 0) >= lax.broadcasted_iota(jnp.int32, (C, C), 1),
                     1.0, 0.0).astype(BF16)
    ti2 = lax.broadcasted_iota(jnp.int32, (C, 2 * C), 0)
    tj2 = lax.broadcasted_iota(jnp.int32, (C, 2 * C), 1)
    right = tj2 >= C
    strict2 = ti2 > (tj2 & (C - 1))
    lower2 = ti2 >= (tj2 & (C - 1))

    def scaled(s):
        lw = lw_ref[s]
        hi = _bf(lw)
        r1 = lw - hi.astype(F32)
        mid = _bf(r1)
        lo = _bf(r1 - mid.astype(F32))
        cum = _dot(tril, hi) + _dot(tril, mid) + _dot(tril, lo)
        total = cum[C - 1:C, :]
        e_inv = jnp.exp(-cum)
        e_tail = jnp.exp(total - cum)
        kk = kk_ref[s].astype(F32)
        bb = b_ref[s].astype(F32)
        kx = k_ref[s].astype(F32)
        vx = v_ref[s]
        if n_pad:
            rowm = lax.broadcasted_iota(jnp.int32, kk.shape, 0) >= n_pad
            kk = jnp.where(rowm, kk, 0.0)
            bb = jnp.where(rowm, bb, 0.0)
            kx = jnp.where(rowm, kx, 0.0)
            vx = jnp.where(rowm, vx, jnp.zeros_like(vx))
        return dict(a=_bf(-kk * jnp.exp(cum - lw)),
                    b=_bf(bb * e_inv), k=_bf(kx * e_inv), r=_bf(r_ref[s].astype(F32) * jnp.exp(cum)),
                    b_hat=_bf(bb * e_tail), k_hat=_bf(kx * e_tail), v=vx, p_end=jnp.exp(total))

    seqs = [scaled(s) for s in range(n_par)]
    pairs = [(s, h) for s in range(n_par) for h in range(heads)]
    col = lambda s, h, name: seqs[s][name][:, h * hd:(h + 1) * hd]
    s0 = [s_scr[s, h] for s, h in pairs]
    s0b = [_bf(v) for v in s0]
    ar = [jnp.concatenate([col(s, h, "a"), col(s, h, "r")], axis=0) for s, h in pairs]
    s4 = [_dot_nt(ar[i], jnp.concatenate([col(s, h, "b"), col(s, h, "k")], axis=0))
          for i, (s, h) in enumerate(pairs)]
    from_state = [_dot_nt(ar[i], s0b[i]) for i in range(len(pairs))]
    top = [jnp.where(strict2, v[:C], 0.0) for v in s4]
    bot = [_bf(jnp.where(lower2, v[C:], 0.0)) for v in s4]
    x0 = [from_state[i][:C] + _dot(_bf(top[i][:, C:]), col(s, h, "v")) for i, (s, h) in enumerate(pairs)]
    z = [jnp.concatenate([top[i][:, :C], x0[i]], axis=1) for i in range(len(pairs))]
    for _ in range(6):
        zb = [_bf(v) for v in z]
        z = [_dot(zb[i][:, :C], zb[i]) + jnp.where(right, z[i], 0.0) for i in range(len(pairs))]
    uv = [jnp.concatenate([_bf(z[i][:, C:]), col(s, h, "v")], axis=0) for i, (s, h) in enumerate(pairs)]
    for i, (s, h) in enumerate(pairs):
        y_ref[s, :, h * hd:(h + 1) * hd] = from_state[i][C:] + _dot(bot[i], uv[i])
    for i, (s, h) in enumerate(pairs):
        s_scr[s, h] = s0[i] * seqs[s]["p_end"][:, h * hd:(h + 1) * hd] + _dot_tn(
            uv[i], jnp.concatenate([col(s, h, "b_hat"), col(s, h, "k_hat")], axis=0))

    @pl.when(c == n_chunks - 1)
    def _():
        sout_ref[...] = s_scr[...]


def _wkv_chunks(r, lw, k, v, kk, b, s0, *, n_par, heads, hd, n_pad):
    n_seq, tokens, rc = r.shape
    n_chunks = tokens // CHUNK
    tile = pl.BlockSpec((n_par, CHUNK, rc), lambda i, c: (i, c, 0))
    state = pl.BlockSpec((n_par, heads, hd, hd), lambda i, c: (i, 0, 0, 0))
    kern = functools.partial(_wkv_chunk_kernel, heads=heads, hd=hd, n_pad=n_pad, n_par=n_par)
    return pl.pallas_call(
        kern,
        out_shape=(jax.ShapeDtypeStruct((n_seq, tokens, rc), F32),
                   jax.ShapeDtypeStruct((n_seq, heads, hd, hd), F32)),
        grid=(n_seq // n_par, n_chunks),
        in_specs=[tile] * 6 + [pl.BlockSpec((heads, hd, hd), lambda i, c: (0, 0, 0))],
        out_specs=(tile, state),
        scratch_shapes=[pltpu.VMEM((n_par, heads, hd, hd), F32)],
        compiler_params=_cparams("parallel", "arbitrary"),
        name="wkv_chunks",
    )(r, lw, k, v, kk, b, s0)


def _wkv_step_kernel(r_ref, lw_ref, k_ref, v_ref, kk_ref, b_ref, s_ref, *rest, hd, slab):
    y_ref, sout_ref = rest[-2:]
    w = jnp.exp(lw_ref[...])
    kk, bb, kx, rx, vx = kk_ref[...], b_ref[...], k_ref[...], r_ref[...], v_ref[...]
    per_slab = slab // hd
    ys = []
    for j in range(hd // per_slab):
        cols = slice(j * slab, (j + 1) * slab)
        st = s_ref[0, :, cols].T
        outs = []
        for t in range(per_slab):
            vi = j * per_slab + t
            s = st[t * hd:(t + 1) * hd, :]
            s_kk = jnp.sum(s * kk, axis=0, keepdims=True)
            s_new = s * w - s_kk * bb + vx[vi:vi + 1, :] * kx
            ys.append(jnp.sum(s_new * rx, axis=0, keepdims=True))
            outs.append(s_new)
        sout_ref[0, :, cols] = jnp.concatenate(outs, axis=0).T
    y_ref[...] = jnp.concatenate(ys, axis=0)


def _wkv_step(r, lw, k, v, kk, b, states, stacked_out, layer, *, heads, hd):
    rc, n_seq = r.shape
    slab = max(hd, 128)
    vec = pl.BlockSpec((hd, n_seq), lambda h: (h, 0))
    st_spec = pl.BlockSpec((1, n_seq, hd * hd), lambda h: (layer, 0, h))
    kern = functools.partial(_wkv_step_kernel, hd=hd, slab=slab)
    extra, extra_specs, aliases = _stacked_alias(stacked_out, 7)
    return pl.pallas_call(
        kern,
        out_shape=(jax.ShapeDtypeStruct((rc, n_seq), F32), jax.ShapeDtypeStruct(states.shape, F32)),
        grid=(heads,),
        in_specs=[vec] * 6 + [st_spec] + extra_specs,
        out_specs=(vec, st_spec),
        input_output_aliases=aliases,
        compiler_params=_cparams("parallel"),
        name="wkv_step",
    )(r, lw, k, v, kk, b, states, *extra)


def _mix_kernel(og_ref, y_ref, r_ref, k_ref, v_ref, g_ref, gates_ref, x_ref,
                wret_ref, wrw_ref, wout_ref, lnw_ref, lnb_ref, rk_ref, gain_ref, o_ref, *, hd, d):
    y_ret = _dot(og_ref[...], wret_ref[...])
    ones = _group_ones(hd)
    y = y_ref[...]
    inv_n = 1.0 / hd
    mean = _group_sum(y, ones, split=False) * inv_n
    dlt = y - mean
    var = _group_sum(dlt * dlt, ones, split=False) * inv_n
    yn = dlt * lax.rsqrt(var + LNX_EPS) * lnw_ref[...] + lnb_ref[...]
    rkk = r_ref[...].astype(F32) * k_ref[...].astype(F32) * rk_ref[...]
    bonus = _group_sum(rkk, ones, split=False) * v_ref[...].astype(F32)
    y_rw = _dot(_bf((yn + bonus) * g_ref[...].astype(F32)), wrw_ref[...])
    gates = gates_ref[...].astype(F32)
    mix = _sigmoid(gates[:, :d]) * y_ret + _sigmoid(gates[:, d:]) * y_rw
    o_ref[...] = x_ref[...] + _rms(_dot(_bf(mix), wout_ref[...]), gain_ref[...])


def _mix(og, y, r, k, v, g, proj, x, w_ret, w_rw, w_out, lnw, lnb, rk, gain, *, layer, tm, hd, gate_block):
    rows, d = x.shape
    tile = lambda a: pl.BlockSpec((tm, a.shape[1]), lambda i: (i, 0))
    const = lambda a: pl.BlockSpec(a.shape, lambda i: (0,) * a.ndim)
    stacked = lambda a: pl.BlockSpec((None,) + a.shape[1:], lambda i: (layer, 0, 0),
                                     pipeline_mode=pl.Buffered(1))
    kern = functools.partial(_mix_kernel, hd=hd, d=d)
    return pl.pallas_call(
        kern,
        out_shape=jax.ShapeDtypeStruct((rows, d), F32),
        grid=(rows // tm,),
        in_specs=[tile(og), tile(y), tile(r), tile(k), tile(v), tile(g),
                  pl.BlockSpec((tm, 2 * d), lambda i: (i, gate_block)), tile(x),
                  stacked(w_ret), stacked(w_rw), stacked(w_out), const(lnw), const(lnb), const(rk), const(gain)],
        out_specs=pl.BlockSpec((tm, d), lambda i: (i, 0)),
        compiler_params=_cparams("parallel"),
        name="mix",
    )(og, y, r, k, v, g, proj, x, w_ret, w_rw, w_out, lnw, lnb, rk, gain)


def _ffn_kernel(x_ref, g2_ref, g3_ref, w1_ref, w3_ref, w2_ref, o_ref, hn_ref, acc_ref):
    j = pl.program_id(1)

    @pl.when(j == 0)
    def _():
        hn_ref[...] = _bf(_rms(x_ref[...], g2_ref[...]))
        acc_ref[...] = jnp.zeros_like(acc_ref)

    tm = hn_ref.shape[0]
    sub = FFN_SUB_ROWS if tm % FFN_SUB_ROWS == 0 else tm
    blocks = [slice(r, r + sub) for r in range(0, tm, sub)]

    def up(rows):
        hn = hn_ref[rows, :]
        return _dot(hn, w1_ref[...]), _dot(hn, w3_ref[...])

    def down(rows, ab):
        a, b = ab
        acc_ref[rows, :] += _dot(_bf((a * _sigmoid(a)) * b), w2_ref[...])

    pending = up(blocks[0])
    for r in range(1, len(blocks)):
        nxt = up(blocks[r])
        down(blocks[r - 1], pending)
        pending = nxt
    down(blocks[-1], pending)

    @pl.when(j == pl.num_programs(1) - 1)
    def _():
        o_ref[...] = x_ref[...] + _rms(acc_ref[...], g3_ref[...])


def _ffn(x, gain2, gain3, w1, w3, w2, *, layer, tm, tf):
    rows, d = x.shape
    f = w1.shape[2]
    mode = dict(pipeline_mode=pl.Buffered(1)) if tf == f else {}
    return pl.pallas_call(
        _ffn_kernel,
        out_shape=jax.ShapeDtypeStruct((rows, d), F32),
        grid=(rows // tm, f // tf),
        in_specs=[pl.BlockSpec((tm, d), lambda i, j: (i, 0)),
                  pl.BlockSpec((1, d), lambda i, j: (0, 0)),
                  pl.BlockSpec((1, d), lambda i, j: (0, 0)),
                  pl.BlockSpec((None, d, tf), lambda i, j: (layer, 0, j), **mode),
                  pl.BlockSpec((None, d, tf), lambda i, j: (layer, 0, j), **mode),
                  pl.BlockSpec((None, tf, d), lambda i, j: (layer, j, 0), **mode)],
        out_specs=pl.BlockSpec((tm, d), lambda i, j: (i, 0)),
        scratch_shapes=[pltpu.VMEM((tm, d), BF16), pltpu.VMEM((tm, d), F32)],
        compiler_params=_cparams("parallel", "arbitrary"),
        name="ffn",
    )(x, gain2, gain3, w1, w3, w2)


def _pick(n, prefs):
    for t in prefs:
        if n % t == 0:
            return t
    return n


def _rope_tables(pos, half):
    inv_freq = 1.0 / (ROPE_BASE ** (jnp.arange(half, dtype=F32) / half))
    ang = pos.astype(F32)[:, None] * inv_freq[None, :]
    return jnp.cos(ang), jnp.sin(ang)


def kernel(x_prompt, x_sample, state_ret, state_wkv, state_shift, meta_tokens, norm_gain, w_in, w_ret_out,
           w_rwkv_out, w_out, rwkv_mu, rwkv_w0, rwkv_w2, rwkv_a0, rwkv_a2, rwkv_g2, rwkv_kk, rwkv_ka, rwkv_rk,
           rwkv_lnx_w, rwkv_lnx_b, rwkv_v0, rwkv_v1, rwkv_v2, ffn_w1, ffn_w3, ffn_w2):
    bp, sp, d = x_prompt.shape
    ns = x_sample.shape[0]
    depth = w_in.shape[0]
    n_meta = meta_tokens.shape[0]
    _, _, rh, dk, dv = state_ret.shape
    _, _, wh, hd, _ = state_wkv.shape
    qk, vw, rc = rh * dk, rh * dv, wh * hd
    rw_w = state_shift.shape[-1]
    assert x_sample.shape[1] == 1 and n_meta <= CHUNK and sp % CHUNK == 0
    assert vw == 2 * qk and d == qk and rc == d and hd & (hd - 1) == 0 and MXU_TILE % hd == 0
    n_chunks = sp // CHUNK
    n_pad = CHUNK - n_meta
    main_rows = bp * sp
    small_rows = CHUNK + ns
    half = dk // 2

    xm = x_prompt.reshape(main_rows, d)
    xs = jnp.concatenate([jnp.zeros((n_pad, d), F32), meta_tokens.astype(F32), x_sample.reshape(ns, d)], axis=0)

    cos_m, sin_m = _rope_tables(n_meta + jnp.arange(sp), half)
    cos_t, sin_t = _rope_tables(jnp.arange(CHUNK) - n_pad, half)
    cos_s, sin_s = _rope_tables(jnp.full((1,), PAST_LEN), half)

    split = 2 * qk + 2 * vw
    wa = jnp.concatenate([_bf(w_in[:, :, :split]), _bf(w_in[:, :, split + rw_w:])], axis=-1)
    wb = _bf(w_in[:, :, split:split + rw_w])
    gate_block = split // (2 * d)
    assert gate_block * 2 * d == split

    tm_m = _pick(main_rows, (1024, 512, 256, 128, 64))
    tm_mix = _pick(main_rows, (512, 256, 128, 64))
    ret_chunk = _pick(sp, (4 * CHUNK, 2 * CHUNK, CHUNK))
    tm_proj = _pick(main_rows, (1024, 512, 256, 128, 64))
    tn_a = _pick(wa.shape[-1], (2048, 1024, 512, 256))
    tf = ffn_w1.shape[-1]
    seq_tm = _pick(sp, PREP_TILES)

    w_ret_b, w_rw_b, w_out_b = _bf(w_ret_out), _bf(w_rwkv_out), _bf(w_out)
    w1_b, w3_b, w2_b = _bf(ffn_w1), _bf(ffn_w3), _bf(ffn_w2)
    ret_p, wkv_p, sh_p, sh_s = [], [], [], []
    ret_s = wkv_s = None
    wkv_flat = state_wkv.reshape(depth, ns, wh * hd * hd)
    vf_m = vf_s = None
    for l in range(depth):
        g = norm_gain[l]
        p = dict(mu=rwkv_mu[l][None], w0=rwkv_w0[l][None], w2=_bf(rwkv_w2[l]), a0=rwkv_a0[l][None],
                 a2=_bf(rwkv_a2[l]), g2=_bf(rwkv_g2[l]), kk=rwkv_kk[l][None], ka=rwkv_ka[l][None])
        lv = max(l - 1, 0)
        p.update(v0=rwkv_v0[lv][None], v1=_bf(rwkv_v1[lv]), v2=_bf(rwkv_v2[lv]))

        pa_s = _norm_proj(xs, g[0][None], wa, l, small_rows, tn_a)
        prev_s = jnp.concatenate([jnp.zeros((CHUNK, rw_w), F32), state_shift[l]], axis=0)
        if l == 0:
            vf_s = jnp.zeros((small_rows, rc), BF16)
        r_s, lw_s, k_s, v_s, kk_s, b_s, g_s, pb_sf = _proj_prep(
            xs, g[0][None], wb, l, prev_s, p, vf_s, layer0=(l == 0), small=True, tm=small_rows,
            seq_rows=small_rows, hd=hd, n_pad=n_pad, n_head=CHUNK)
        if l == 0:
            vf_s = v_s

        og_meta, s_ret_meta = _ret_chunks(pa_s, cos_t, sin_t, jnp.zeros((rh, dk, dv), F32), chunk=CHUNK,
                                          n_seq=1, n_chunks=1, heads=rh, dk=dk, dv=dv, n_pad=n_pad)
        meta_in = [a[:CHUNK].reshape(1, CHUNK, rc) for a in (r_s, lw_s, k_s, v_s, kk_s, b_s)]
        y_meta, s_wkv_meta = _wkv_chunks(*meta_in, jnp.zeros((wh, hd, hd), F32), n_par=1, heads=wh, hd=hd,
                                         n_pad=n_pad)
        y_meta = y_meta.reshape(CHUNK, rc)
        pa_sf = pa_s[CHUNK:].astype(F32)
        og_smp, ret_s = _ret_step(pa_sf[:, :qk], pa_sf[:, qk:2 * qk], pa_sf[:, 2 * qk:2 * qk + vw],
                                  pa_sf[:, 2 * qk + vw:split], cos_s, sin_s, state_ret, ret_s, l,
                                  heads=rh, dk=dk, dv=dv, nb=_pick(ns, (2, 1)))
        smp = [a[CHUNK:].astype(F32).T for a in (r_s, lw_s, k_s, v_s, kk_s, b_s)]
        y_smp, wkv_s = _wkv_step(*smp, wkv_flat, wkv_s, l, heads=wh, hd=hd)
        og_s = jnp.concatenate([og_meta, _bf(og_smp)], axis=0)
        y_s = jnp.concatenate([y_meta, y_smp.T], axis=0)
        sh_s.append(pb_sf[CHUNK:])

        pa_m = _norm_proj(xm, g[0][None], wa, l, tm_proj, tn_a)
        if l == 0:
            vf_m = jnp.zeros((main_rows, rc), BF16)
        r_m, lw_m, k_m, v_m, kk_m, b_m, g_m, tails = _proj_prep(
            xm, g[0][None], wb, l, pb_sf[CHUNK - 1:CHUNK], p, vf_m, layer0=(l == 0), small=False, tm=seq_tm,
            seq_rows=sp, hd=hd)
        if l == 0:
            vf_m = v_m
        og_m, s_ret_m = _ret_chunks(pa_m, cos_m, sin_m, s_ret_meta[0], chunk=ret_chunk, n_seq=bp,
                                    n_chunks=sp // ret_chunk, heads=rh, dk=dk, dv=dv, n_pad=0)
        main_in = [a.reshape(bp, sp, rc) for a in (r_m, lw_m, k_m, v_m, kk_m, b_m)]
        y_m, s_wkv_m = _wkv_chunks(*main_in, s_wkv_meta[0], n_par=_pick(bp, (WKV_PAR, 1)), heads=wh, hd=hd,
                                   n_pad=0)
        y_m = y_m.reshape(main_rows, rc)
        ret_p.append(s_ret_m)
        wkv_p.append(s_wkv_m)
        sh_p.append(tails.reshape(bp, sp // seq_tm, rw_w)[:, -1])

        mixw = (w_ret_b, w_rw_b, w_out_b, rwkv_lnx_w[l][None], rwkv_lnx_b[l][None], rwkv_rk[l][None], g[1][None])
        ffnw = (g[2][None], g[3][None], w1_b, w3_b, w2_b)
        mix_kw = dict(layer=l, hd=hd, gate_block=gate_block)
        xs = _mix(og_s, y_s, r_s, k_s, v_s, g_s, pa_s, xs, *mixw, tm=small_rows, **mix_kw)
        xs = _ffn(xs, *ffnw, layer=l, tm=small_rows, tf=tf)
        xm = _mix(og_m, y_m, r_m, k_m, v_m, g_m, pa_m, xm, *mixw, tm=tm_mix, **mix_kw)
        xm = _ffn(xm, *ffnw, layer=l, tm=tm_m, tf=tf)

    return (xm.reshape(bp, sp, d), xs[CHUNK:].reshape(ns, 1, d), jnp.stack(ret_p), jnp.stack(wkv_p),
            jnp.stack(sh_p), ret_s, wkv_s.reshape(state_wkv.shape), jnp.stack(sh_s))
```

```python
import functools
import math

import jax
import jax.numpy as jnp
from jax import lax
from jax.experimental import pallas as pl
from jax.experimental.pallas import tpu as pltpu

F32 = jnp.float32
BF16 = jnp.bfloat16

NORM_EPS = 1e-6
LNX_EPS = 64e-5
ROPE_BASE = 10000.0
PAST_LEN = 16384
KK_EPS = 1e-12

CHUNK = 64
MXU_TILE = 256
VMEM_LIMIT = 56 * 1024 * 1024
PREP_TILES = (512, 256, 128, 64)
FFN_SUB_ROWS = 256
WKV_PAR = 4


def _cparams(*sem):
    return pltpu.CompilerParams(dimension_semantics=sem, vmem_limit_bytes=VMEM_LIMIT)


def _dot(a, b):
    return jnp.dot(a, b, preferred_element_type=F32)


def _dot_nt(a, b):
    return lax.dot_general(a, b, (((1,), (1,)), ((), ())), preferred_element_type=F32)


def _dot_tn(a, b):
    return lax.dot_general(a, b, (((0,), (0,)), ((), ())), preferred_element_type=F32)


def _bf(x):
    return x.astype(BF16)


def _sigmoid(x):
    return 0.5 * jnp.tanh(0.5 * x) + 0.5


def _rms(x, gain):
    return x * lax.rsqrt(jnp.mean(x * x, axis=-1, keepdims=True) + NORM_EPS) * gain


def _group_ones(group):
    shift = group.bit_length() - 1
    r = lax.broadcasted_iota(jnp.int32, (MXU_TILE, MXU_TILE), 0) >> shift
    c = lax.broadcasted_iota(jnp.int32, (MXU_TILE, MXU_TILE), 1) >> shift
    return jnp.where(r == c, 1.0, 0.0).astype(BF16)


def _group_sum(x, ones, split=True):
    hi = _bf(x)
    lo = _bf(x - hi.astype(F32)) if split else None
    parts = []
    for j in range(x.shape[1] // MXU_TILE):
        sl = slice(j * MXU_TILE, (j + 1) * MXU_TILE)
        part = _dot(hi[:, sl], ones)
        parts.append(part + _dot(lo[:, sl], ones) if split else part)
    return parts[0] if len(parts) == 1 else jnp.concatenate(parts, axis=1)


def _norm_proj_kernel(x_ref, g_ref, w_ref, o_ref):
    o_ref[...] = _dot(_bf(_rms(x_ref[...], g_ref[...])), w_ref[...]).astype(o_ref.dtype)


def _norm_proj(x, gain, w, layer, tm, tn):
    rows, d = x.shape
    n = w.shape[2]
    return pl.pallas_call(
        _norm_proj_kernel,
        out_shape=jax.ShapeDtypeStruct((rows, n), BF16),
        grid=(n // tn, rows // tm),
        in_specs=[pl.BlockSpec((tm, d), lambda j, i: (i, 0)),
                  pl.BlockSpec((1, d), lambda j, i: (0, 0)),
                  pl.BlockSpec((None, d, tn), lambda j, i: (layer, 0, j))],
        out_specs=pl.BlockSpec((tm, tn), lambda j, i: (i, j)),
        compiler_params=_cparams("parallel", "parallel"),
        name="norm_proj",
    )(x, gain, w)


def _ret_chunk_kernel(q_ref, k_ref, v_ref, gr_ref, cos_ref, sin_ref, dmask_ref, s0_ref, og_ref, sout_ref, s_scr,
                      *, heads, dk, dv, n_pad):
    c = pl.program_id(1)
    n_chunks = pl.num_programs(1)
    half = dk // 2

    @pl.when(c == 0)
    def _():
        s_scr[...] = s0_ref[...]

    chunk = q_ref.shape[0]
    cos = cos_ref[...]
    sin = sin_ref[...]
    row = lax.broadcasted_iota(jnp.int32, (chunk, half), 0)
    rowf = row.astype(F32)
    cos_k = cos * (dk ** -0.5)
    sin_k = sin * (dk ** -0.5)

    def rope(x, cs, sn):
        x1 = x[:, :half]
        x2 = x[:, half:]
        return x1 * cs - x2 * sn, x2 * cs + x1 * sn

    hr = range(heads)
    lgs = [math.log(1.0 - 2.0 ** (-5.0 - h)) for h in hr]
    vsl = [slice(h * dv, (h + 1) * dv) for h in hr]
    qb, kb, qd, kd, vhs, dmask = [], [], [], [], [], []
    for h in hr:
        lg = lgs[h]
        qs = slice(h * dk, (h + 1) * dk)
        q1, q2 = rope(q_ref[:, qs].astype(F32), cos, sin)
        k1, k2 = rope(k_ref[:, qs].astype(F32), cos_k, sin_k)
        vh = v_ref[:, vsl[h]]
        if n_pad:
            k1 = jnp.where(row >= n_pad, k1, 0.0)
            k2 = jnp.where(row >= n_pad, k2, 0.0)
            vrow = lax.broadcasted_iota(jnp.int32, (chunk, dv), 0)
            vh = jnp.where(vrow >= n_pad, vh, jnp.zeros_like(vh))
        q_decay = jnp.exp((rowf + 1.0) * lg)
        k_decay = jnp.exp((chunk - 1.0 - rowf) * lg)
        dmask.append(dmask_ref[h])
        qb.append(_bf(jnp.concatenate([q1, q2], axis=1)))
        kb.append(_bf(jnp.concatenate([k1, k2], axis=1)))
        qd.append(_bf(jnp.concatenate([q1 * q_decay, q2 * q_decay], axis=1)))
        kd.append(_bf(jnp.concatenate([k1 * k_decay, k2 * k_decay], axis=1)))
        vhs.append(vh)
    s_old = [s_scr[h] for h in hr]
    scores = [_bf(_dot_nt(qb[h], kb[h]) * dmask[h]) for h in hr]
    cross = [_dot(qd[h], _bf(s_old[h])) for h in hr]
    inner = [_dot(scores[h], vhs[h]) for h in hr]
    for h in hr:
        o = inner[h] + cross[h]
        o = o * lax.rsqrt(jnp.mean(o * o, axis=-1, keepdims=True) + NORM_EPS)
        g = gr_ref[:, vsl[h]].astype(F32)
        og_ref[:, vsl[h]] = _bf(o * (g * _sigmoid(g)))
    for h in hr:
        s_scr[h] = math.exp(chunk * lgs[h]) * s_old[h] + _dot_tn(kd[h], vhs[h])

    @pl.when(c == n_chunks - 1)
    def _():
        sout_ref[0] = s_scr[...]


def _decay_mask(chunk, heads):
    rel = (jnp.arange(chunk)[:, None] - jnp.arange(chunk)[None, :]).astype(F32)
    lg = jnp.asarray([math.log(1.0 - 2.0 ** (-5.0 - h)) for h in range(heads)], F32)[:, None, None]
    return jnp.where(rel[None] >= 0, jnp.exp(lg * jnp.maximum(rel, 0.0)[None]), 0.0)


def _ret_chunks(proj, cos, sin, s0, *, chunk, n_seq, n_chunks, heads, dk, dv, n_pad):
    qk = heads * dk
    vw = heads * dv
    assert vw == 2 * qk

    def rowmap(col):
        return lambda b, c: (b * n_chunks + c, col)

    kern = functools.partial(_ret_chunk_kernel, heads=heads, dk=dk, dv=dv, n_pad=n_pad)
    return pl.pallas_call(
        kern,
        out_shape=(jax.ShapeDtypeStruct((n_seq * n_chunks * chunk, vw), BF16),
                   jax.ShapeDtypeStruct((n_seq, heads, dk, dv), F32)),
        grid=(n_seq, n_chunks),
        in_specs=[pl.BlockSpec((chunk, qk), rowmap(0)),
                  pl.BlockSpec((chunk, qk), rowmap(1)),
                  pl.BlockSpec((chunk, vw), rowmap(1)),
                  pl.BlockSpec((chunk, vw), rowmap(2)),
                  pl.BlockSpec((chunk, dk // 2), lambda b, c: (c, 0)),
                  pl.BlockSpec((chunk, dk // 2), lambda b, c: (c, 0)),
                  pl.BlockSpec((heads, chunk, chunk), lambda b, c: (0, 0, 0)),
                  pl.BlockSpec((heads, dk, dv), lambda b, c: (0, 0, 0))],
        out_specs=(pl.BlockSpec((chunk, vw), lambda b, c: (b * n_chunks + c, 0)),
                   pl.BlockSpec((1, heads, dk, dv), lambda b, c: (b, 0, 0, 0))),
        scratch_shapes=[pltpu.VMEM((heads, dk, dv), F32)],
        compiler_params=_cparams("parallel", "arbitrary"),
        name="ret_chunks",
    )(proj, proj, proj, proj, cos, sin, _decay_mask(chunk, heads), s0)


def _ret_step_kernel(q_ref, k_ref, v_ref, gr_ref, cos_ref, sin_ref, s_ref, *rest, heads, dk, dv, nb):
    og_ref, sout_ref = rest[-2:]
    half = dk // 2
    cos = cos_ref[...]
    sin = sin_ref[...]
    base = pl.program_id(0) * nb
    pad_rows = 16
    lane_w = 128
    sel_r = lax.broadcasted_iota(jnp.int32, (pad_rows, 2 * lane_w), 0)
    sel_c = lax.broadcasted_iota(jnp.int32, (pad_rows, 2 * lane_w), 1)
    spread = jnp.where((sel_r == 0) & (sel_c < lane_w) | (sel_r == 1) & (sel_c >= lane_w), 1.0, 0.0).astype(BF16)
    row_id = lax.broadcasted_iota(jnp.int32, (pad_rows, 1), 0)

    def rope(x):
        x1 = x[:, :half]
        x2 = x[:, half:]
        return jnp.concatenate([x1 * cos - x2 * sin, x2 * cos + x1 * sin], axis=1)

    for i in range(nb):
        n = base + i
        q_row, k_row, v_row, g_row = (ref[pl.ds(n, 1), :] for ref in (q_ref, k_ref, v_ref, gr_ref))
        o_parts = []
        for h in range(heads):
            gamma = 1.0 - 2.0 ** (-5.0 - h)
            qs = slice(h * dk, (h + 1) * dk)
            vs = slice(h * dv, (h + 1) * dv)
            q = rope(q_row[:, qs])
            k = rope(k_row[:, qs]) * (dk ** -0.5)
            v = v_row[:, vs]
            kq_rows = jnp.where(row_id == 0, jnp.broadcast_to(k, (pad_rows, dk)),
                                jnp.where(row_id == 1, jnp.broadcast_to(q, (pad_rows, dk)), 0.0))
            cols = _dot_tn(_bf(kq_rows), spread)
            k_col, q_col = cols[:, :lane_w], cols[:, lane_w:]
            o_tiles = []
            for t in range(dv // lane_w):
                ts = slice(t * lane_w, (t + 1) * lane_w)
                s_new = gamma * s_ref[0, i, h, :, ts] + k_col * v[:, ts]
                sout_ref[0, i, h, :, ts] = s_new
                o_tiles.append(jnp.sum(q_col * s_new, axis=0, keepdims=True))
            o = jnp.concatenate(o_tiles, axis=1)
            o = o * lax.rsqrt(jnp.mean(o * o, axis=-1, keepdims=True) + NORM_EPS)
            g = g_row[:, vs]
            o_parts.append(o * (g * _sigmoid(g)))
        og_ref[pl.ds(n, 1), :] = jnp.concatenate(o_parts, axis=1)


def _stacked_alias(stacked_out, n_inputs):
    if stacked_out is None:
        return [], [], {}
    return [stacked_out], [pl.BlockSpec(memory_space=pl.ANY)], {n_inputs: 1}


def _ret_step(q, k, v, gr, cos, sin, states, stacked_out, layer, *, heads, dk, dv, nb):
    n_seq = q.shape[0]
    full = lambda a: pl.BlockSpec(a.shape, lambda i: (0,) * a.ndim)
    kern = functools.partial(_ret_step_kernel, heads=heads, dk=dk, dv=dv, nb=nb)
    st_spec = pl.BlockSpec((1, nb, heads, dk, dv), lambda i: (layer, i, 0, 0, 0))
    extra, extra_specs, aliases = _stacked_alias(stacked_out, 7)
    return pl.pallas_call(
        kern,
        out_shape=(jax.ShapeDtypeStruct((n_seq, heads * dv), F32),
                   jax.ShapeDtypeStruct(states.shape, F32)),
        grid=(n_seq // nb,),
        in_specs=[full(q), full(k), full(v), full(gr), full(cos), full(sin), st_spec] + extra_specs,
        out_specs=(pl.BlockSpec((n_seq, heads * dv), lambda i: (0, 0)), st_spec),
        input_output_aliases=aliases,
        compiler_params=_cparams("arbitrary"),
        name="ret_step",
    )(q, k, v, gr, cos, sin, states, *extra)


def _proj_prep_kernel(*refs, layer0, small, tm, rc, hd, lw_, la_, seq_tiles, n_pad, n_head):
    (x_ref, gain_ref, wb_ref, lead_ref, mu_ref, w0_ref, w2_ref, a0_ref, a2_ref, g2_ref, kkp_ref, kap_ref,
     v0_ref, v1_ref, v2_ref, vf_ref, r_o, lw_o, k_o, v_o, kk_o, b_o, g_o, rw_o) = refs[:24]
    rw = _dot(_bf(_rms(x_ref[...], gain_ref[...])), wb_ref[...])
    rolled = pltpu.roll(rw, 1, 0)
    if small:
        row = lax.broadcasted_iota(jnp.int32, (tm, 1), 0)
        prev = jnp.where((row <= n_pad) | (row >= n_head), lead_ref[...], rolled)
        rw_o[...] = rw
    else:
        carry_ref = refs[24]
        is_start = (pl.program_id(0) % seq_tiles) == 0
        first = jnp.where(is_start, lead_ref[...], carry_ref[...])
        row0 = lax.broadcasted_iota(jnp.int32, (8, 1), 0) == 0
        prev = jnp.concatenate([jnp.where(row0, first, rolled[:8]), rolled[8:]], axis=0)
        carry_ref[...] = rw[tm - 1:tm]
        rw_o[0] = rw[tm - 1:tm]
    z = rw + (prev - rw) * mu_ref[...]

    z_l = z[:, 3 * rc:]
    wd = z_l[:, :lw_]
    ad = z_l[:, lw_:lw_ + la_]
    gd = z_l[:, lw_ + la_:]
    w_in = w0_ref[...] + _dot(_bf(jnp.tanh(wd)), w2_ref[...])
    lw_o[...] = (-math.exp(-0.5)) * _sigmoid(w_in)
    a = _sigmoid(a0_ref[...] + _dot(_bf(ad), a2_ref[...]))
    g_o[...] = _bf(_dot(_bf(_sigmoid(gd)), g2_ref[...]))

    r_o[...] = _bf(z[:, :rc])

    z_k = z[:, rc:2 * rc]
    kk = z_k * kkp_ref[...]
    ones = _group_ones(hd)
    kk = kk * lax.rsqrt(jnp.maximum(_group_sum(kk * kk, ones, split=False), KK_EPS * KK_EPS))
    kk_o[...] = _bf(kk)
    b_o[...] = _bf(kk * a)
    k_o[...] = _bf(z_k * (1.0 + (a - 1.0) * kap_ref[...]))

    z_v = z[:, 2 * rc:3 * rc]
    if layer0:
        v_o[...] = _bf(z_v)
    else:
        lora = _dot(_bf(_dot(_bf(z_v), v1_ref[...])), v2_ref[...])
        v_o[...] = _bf(z_v + (vf_ref[...].astype(F32) - z_v) * _sigmoid(v0_ref[...] + lora))


def _proj_prep(x, gain, wb, layer, lead, p, v_first, *, layer0, small, tm, seq_rows, hd, n_pad=0, n_head=0):
    rows, d = x.shape
    width = wb.shape[2]
    rc = p["w0"].shape[1]
    lw_ = p["w2"].shape[0]
    la_ = p["a2"].shape[0]
    n_tiles = rows // tm
    assert (n_tiles == 1) if small else (tm % 8 == 0 and seq_rows % tm == 0)
    tile = lambda w: pl.BlockSpec((tm, w), lambda i: (i, 0))
    const = lambda a: pl.BlockSpec(a.shape, lambda i: (0,) * a.ndim)
    params = [p["mu"], p["w0"], p["w2"], p["a0"], p["a2"], p["g2"], p["kk"], p["ka"], p["v0"], p["v1"], p["v2"]]
    kern = functools.partial(_proj_prep_kernel, layer0=layer0, small=small, tm=tm, rc=rc, hd=hd, lw_=lw_,
                             la_=la_, seq_tiles=max(seq_rows // tm, 1), n_pad=n_pad, n_head=n_head)
    out_bf = jax.ShapeDtypeStruct((rows, rc), BF16)
    if small:
        rw_shape, rw_spec, scratch = (rows, width), tile(width), []
    else:
        rw_shape, rw_spec = (n_tiles, 1, width), pl.BlockSpec((1, 1, width), lambda i: (i, 0, 0))
        scratch = [pltpu.VMEM((1, width), F32)]
    return pl.pallas_call(
        kern,
        out_shape=(out_bf, jax.ShapeDtypeStruct((rows, rc), F32), out_bf, out_bf, out_bf, out_bf, out_bf,
                   jax.ShapeDtypeStruct(rw_shape, F32)),
        grid=(n_tiles,),
        in_specs=[tile(d), const(gain),
                  pl.BlockSpec((None, d, width), lambda i: (layer, 0, 0), pipeline_mode=pl.Buffered(1)),
                  tile(width) if small else const(lead)] + [const(a) for a in params] + [tile(rc)],
        out_specs=tuple(tile(rc) for _ in range(7)) + (rw_spec,),
        scratch_shapes=scratch,
        compiler_params=_cparams("arbitrary"),
        name="proj_prep",
    )(x, gain, wb, lead, *params, v_first)


def _wkv_chunk_kernel(r_ref, lw_ref, k_ref, v_ref, kk_ref, b_ref, s0_ref, y_ref, sout_ref, s_scr,
                      *, heads, hd, n_pad, n_par):
    c = pl.program_id(1)
    n_chunks = pl.num_programs(1)
    C = CHUNK

    @pl.when(c == 0)
    def _():
        for s in range(n_par):
            s_scr[s] = s0_ref[...]

    tril = jnp.where(lax.broadcasted_iota(jnp.int32, (C, C), 0) >= lax.broadcasted_iota(jnp.int32, (C, C), 1),
                     1.0, 0.0).astype(BF16)
    ti2 = lax.broadcasted_iota(jnp.int32, (C, 2 * C), 0)
    tj2 = lax.broadcasted_iota(jnp.int32, (C, 2 * C), 1)
    right = tj2 >= C
    strict2 = ti2 > (tj2 & (C - 1))
    lower2 = ti2 >= (tj2 & (C - 1))

    def scaled(s):
        lw = lw_ref[s]
        hi = _bf(lw)
        r1 = lw - hi.astype(F32)
        mid = _bf(r1)
        lo = _bf(r1 - mid.astype(F32))
        cum = _dot(tril, hi) + _dot(tril, mid) + _dot(tril, lo)
        total = cum[C - 1:C, :]
        e_inv = jnp.exp(-cum)
        e_tail = jnp.exp(total - cum)
        kk = kk_ref[s].astype(F32)
        bb = b_ref[s].astype(F32)
        kx = k_ref[s].astype(F32)
        vx = v_ref[s]
        if n_pad:
            rowm = lax.broadcasted_iota(jnp.int32, kk.shape, 0) >= n_pad
            kk = jnp.where(rowm, kk, 0.0)
            bb = jnp.where(rowm, bb, 0.0)
            kx = jnp.where(rowm, kx, 0.0)
            vx = jnp.where(rowm, vx, jnp.zeros_like(vx))
        return dict(a=_bf(-kk * jnp.exp(cum - lw)),
                    b=_bf(bb * e_inv), k=_bf(kx * e_inv), r=_bf(r_ref[s].astype(F32) * jnp.exp(cum)),
                    b_hat=_bf(bb * e_tail), k_hat=_bf(kx * e_tail), v=vx, p_end=jnp.exp(total))

    seqs = [scaled(s) for s in range(n_par)]
    pairs = [(s, h) for s in range(n_par) for h in range(heads)]
    col = lambda s, h, name: seqs[s][name][:, h * hd:(h + 1) * hd]
    s0 = [s_scr[s, h] for s, h in pairs]
    s0b = [_bf(v) for v in s0]
    ar = [jnp.concatenate([col(s, h, "a"), col(s, h, "r")], axis=0) for s, h in pairs]
    s4 = [_dot_nt(ar[i], jnp.concatenate([col(s, h, "b"), col(s, h, "k")], axis=0))
          for i, (s, h) in enumerate(pairs)]
    from_state = [_dot_nt(ar[i], s0b[i]) for i in range(len(pairs))]
    top = [jnp.where(strict2, v[:C], 0.0) for v in s4]
    bot = [_bf(jnp.where(lower2, v[C:], 0.0)) for v in s4]
    x0 = [from_state[i][:C] + _dot(_bf(top[i][:, C:]), col(s, h, "v")) for i, (s, h) in enumerate(pairs)]
    z = [jnp.concatenate([top[i][:, :C], x0[i]], axis=1) for i in range(len(pairs))]
    for _ in range(6):
        zb = [_bf(v) for v in z]
        z = [_dot(zb[i][:, :C], zb[i]) + jnp.where(right, z[i], 0.0) for i in range(len(pairs))]
    uv = [jnp.concatenate([_bf(z[i][:, C:]), col(s, h, "v")], axis=0) for i, (s, h) in enumerate(pairs)]
    for i, (s, h) in enumerate(pairs):
        y_ref[s, :, h * hd:(h + 1) * hd] = from_state[i][C:] + _dot(bot[i], uv[i])
    for i, (s, h) in enumerate(pairs):
        s_scr[s, h] = s0[i] * seqs[s]["p_end"][:, h * hd:(h + 1) * hd] + _dot_tn(
            uv[i], jnp.concatenate([col(s, h, "b_hat"), col(s, h, "k_hat")], axis=0))

    @pl.when(c == n_chunks - 1)
    def _():
        sout_ref[...] = s_scr[...]


def _wkv_chunks(r, lw, k, v, kk, b, s0, *, n_par, heads, hd, n_pad):
    n_seq, tokens, rc = r.shape
    n_chunks = tokens // CHUNK
    tile = pl.BlockSpec((n_par, CHUNK, rc), lambda i, c: (i, c, 0))
    state = pl.BlockSpec((n_par, heads, hd, hd), lambda i, c: (i, 0, 0, 0))
    kern = functools.partial(_wkv_chunk_kernel, heads=heads, hd=hd, n_pad=n_pad, n_par=n_par)
    return pl.pallas_call(
        kern,
        out_shape=(jax.ShapeDtypeStruct((n_seq, tokens, rc), F32),
                   jax.ShapeDtypeStruct((n_seq, heads, hd, hd), F32)),
        grid=(n_seq // n_par, n_chunks),
        in_specs=[tile] * 6 + [pl.BlockSpec((heads, hd, hd), lambda i, c: (0, 0, 0))],
        out_specs=(tile, state),
        scratch_shapes=[pltpu.VMEM((n_par, heads, hd, hd), F32)],
        compiler_params=_cparams("parallel", "arbitrary"),
        name="wkv_chunks",
    )(r, lw, k, v, kk, b, s0)


def _wkv_step_kernel(r_ref, lw_ref, k_ref, v_ref, kk_ref, b_ref, s_ref, *rest, hd, slab):
    y_ref, sout_ref = rest[-2:]
    w = jnp.exp(lw_ref[...])
    kk, bb, kx, rx, vx = kk_ref[...], b_ref[...], k_ref[...], r_ref[...], v_ref[...]
    per_slab = slab // hd
    ys = []
    for j in range(hd // per_slab):
        cols = slice(j * slab, (j + 1) * slab)
        st = s_ref[0, :, cols].T
        outs = []
        for t in range(per_slab):
            vi = j * per_slab + t
            s = st[t * hd:(t + 1) * hd, :]
            s_kk = jnp.sum(s * kk, axis=0, keepdims=True)
            s_new = s * w - s_kk * bb + vx[vi:vi + 1, :] * kx
            ys.append(jnp.sum(s_new * rx, axis=0, keepdims=True))
            outs.append(s_new)
        sout_ref[0, :, cols] = jnp.concatenate(outs, axis=0).T
    y_ref[...] = jnp.concatenate(ys, axis=0)


def _wkv_step(r, lw, k, v, kk, b, states, stacked_out, layer, *, heads, hd):
    rc, n_seq = r.shape
    slab = max(hd, 128)
    vec = pl.BlockSpec((hd, n_seq), lambda h: (h, 0))
    st_spec = pl.BlockSpec((1, n_seq, hd * hd), lambda h: (layer, 0, h))
    kern = functools.partial(_wkv_step_kernel, hd=hd, slab=slab)
    extra, extra_specs, aliases = _stacked_alias(stacked_out, 7)
    return pl.pallas_call(
        kern,
        out_shape=(jax.ShapeDtypeStruct((rc, n_seq), F32), jax.ShapeDtypeStruct(states.shape, F32)),
        grid=(heads,),
        in_specs=[vec] * 6 + [st_spec] + extra_specs,
        out_specs=(vec, st_spec),
        input_output_aliases=aliases,
        compiler_params=_cparams("parallel"),
        name="wkv_step",
    )(r, lw, k, v, kk, b, states, *extra)


def _mix_kernel(og_ref, y_ref, r_ref, k_ref, v_ref, g_ref, gates_ref, x_ref,
                wret_ref, wrw_ref, wout_ref, lnw_ref, lnb_ref, rk_ref, gain_ref, o_ref, *, hd, d):
    y_ret = _dot(og_ref[...], wret_ref[...])
    ones = _group_ones(hd)
    y = y_ref[...]
    inv_n = 1.0 / hd
    mean = _group_sum(y, ones, split=False) * inv_n
    dlt = y - mean
    var = _group_sum(dlt * dlt, ones, split=False) * inv_n
    yn = dlt * lax.rsqrt(var + LNX_EPS) * lnw_ref[...] + lnb_ref[...]
    rkk = r_ref[...].astype(F32) * k_ref[...].astype(F32) * rk_ref[...]
    bonus = _group_sum(rkk, ones, split=False) * v_ref[...].astype(F32)
    y_rw = _dot(_bf((yn + bonus) * g_ref[...].astype(F32)), wrw_ref[...])
    gates = gates_ref[...].astype(F32)
    mix = _sigmoid(gates[:, :d]) * y_ret + _sigmoid(gates[:, d:]) * y_rw
    o_ref[...] = x_ref[...] + _rms(_dot(_bf(mix), wout_ref[...]), gain_ref[...])


def _mix(og, y, r, k, v, g, proj, x, w_ret, w_rw, w_out, lnw, lnb, rk, gain, *, layer, tm, hd, gate_block):
    rows, d = x.shape
    tile = lambda a: pl.BlockSpec((tm, a.shape[1]), lambda i: (i, 0))
    const = lambda a: pl.BlockSpec(a.shape, lambda i: (0,) * a.ndim)
    stacked = lambda a: pl.BlockSpec((None,) + a.shape[1:], lambda i: (layer, 0, 0),
                                     pipeline_mode=pl.Buffered(1))
    kern = functools.partial(_mix_kernel, hd=hd, d=d)
    return pl.pallas_call(
        kern,
        out_shape=jax.ShapeDtypeStruct((rows, d), F32),
        grid=(rows // tm,),
        in_specs=[tile(og), tile(y), tile(r), tile(k), tile(v), tile(g),
                  pl.BlockSpec((tm, 2 * d), lambda i: (i, gate_block)), tile(x),
                  stacked(w_ret), stacked(w_rw), stacked(w_out), const(lnw), const(lnb), const(rk), const(gain)],
        out_specs=pl.BlockSpec((tm, d), lambda i: (i, 0)),
        compiler_params=_cparams("parallel"),
        name="mix",
    )(og, y, r, k, v, g, proj, x, w_ret, w_rw, w_out, lnw, lnb, rk, gain)


def _ffn_kernel(x_ref, g2_ref, g3_ref, w1_ref, w3_ref, w2_ref, o_ref, hn_ref, acc_ref):
    j = pl.program_id(1)

    @pl.when(j == 0)
    def _():
        hn_ref[...] = _bf(_rms(x_ref[...], g2_ref[...]))
        acc_ref[...] = jnp.zeros_like(acc_ref)

    tm = hn_ref.shape[0]
    sub = FFN_SUB_ROWS if tm % FFN_SUB_ROWS == 0 else tm
    blocks = [slice(r, r + sub) for r in range(0, tm, sub)]

    def up(rows):
        hn = hn_ref[rows, :]
        return _dot(hn, w1_ref[...]), _dot(hn, w3_ref[...])

    def down(rows, ab):
        a, b = ab
        acc_ref[rows, :] += _dot(_bf((a * _sigmoid(a)) * b), w2_ref[...])

    pending = up(blocks[0])
    for r in range(1, len(blocks)):
        nxt = up(blocks[r])
        down(blocks[r - 1], pending)
        pending = nxt
    down(blocks[-1], pending)

    @pl.when(j == pl.num_programs(1) - 1)
    def _():
        o_ref[...] = x_ref[...] + _rms(acc_ref[...], g3_ref[...])


def _ffn(x, gain2, gain3, w1, w3, w2, *, layer, tm, tf):
    rows, d = x.shape
    f = w1.shape[2]
    mode = dict(pipeline_mode=pl.Buffered(1)) if tf == f else {}
    return pl.pallas_call(
        _ffn_kernel,
        out_shape=jax.ShapeDtypeStruct((rows, d), F32),
        grid=(rows // tm, f // tf),
        in_specs=[pl.BlockSpec((tm, d), lambda i, j: (i, 0)),
                  pl.BlockSpec((1, d), lambda i, j: (0, 0)),
                  pl.BlockSpec((1, d), lambda i, j: (0, 0)),
                  pl.BlockSpec((None, d, tf), lambda i, j: (layer, 0, j), **mode),
                  pl.BlockSpec((None, d, tf), lambda i, j: (layer, 0, j), **mode),
                  pl.BlockSpec((None, tf, d), lambda i, j: (layer, j, 0), **mode)],
        out_specs=pl.BlockSpec((tm, d), lambda i, j: (i, 0)),
        scratch_shapes=[pltpu.VMEM((tm, d), BF16), pltpu.VMEM((tm, d), F32)],
        compiler_params=_cparams("parallel", "arbitrary"),
        name="ffn",
    )(x, gain2, gain3, w1, w3, w2)


def _pick(n, prefs):
    for t in prefs:
        if n % t == 0:
            return t
    return n


def _rope_tables(pos, half):
    inv_freq = 1.0 / (ROPE_BASE ** (jnp.arange(half, dtype=F32) / half))
    ang = pos.astype(F32)[:, None] * inv_freq[None, :]
    return jnp.cos(ang), jnp.sin(ang)


def kernel(x_prompt, x_sample, state_ret, state_wkv, state_shift, meta_tokens, norm_gain, w_in, w_ret_out,
           w_rwkv_out, w_out, rwkv_mu, rwkv_w0, rwkv_w2, rwkv_a0, rwkv_a2, rwkv_g2, rwkv_kk, rwkv_ka, rwkv_rk,
           rwkv_lnx_w, rwkv_lnx_b, rwkv_v0, rwkv_v1, rwkv_v2, ffn_w1, ffn_w3, ffn_w2):
    bp, sp, d = x_prompt.shape
    ns = x_sample.shape[0]
    depth = w_in.shape[0]
    n_meta = meta_tokens.shape[0]
    _, _, rh, dk, dv = state_ret.shape
    _, _, wh, hd, _ = state_wkv.shape
    qk, vw, rc = rh * dk, rh * dv, wh * hd
    rw_w = state_shift.shape[-1]
    assert x_sample.shape[1] == 1 and n_meta <= CHUNK and sp % CHUNK == 0
    assert vw == 2 * qk and d == qk and rc == d and hd & (hd - 1) == 0 and MXU_TILE % hd == 0
    n_chunks = sp // CHUNK
    n_pad = CHUNK - n_meta
    main_rows = bp * sp
    small_rows = CHUNK + ns
    half = dk // 2

    xm = x_prompt.reshape(main_rows, d)
    xs = jnp.concatenate([jnp.zeros((n_pad, d), F32), meta_tokens.astype(F32), x_sample.reshape(ns, d)], axis=0)

    cos_m, sin_m = _rope_tables(n_meta + jnp.arange(sp), half)
    cos_t, sin_t = _rope_tables(jnp.arange(CHUNK) - n_pad, half)
    cos_s, sin_s = _rope_tables(jnp.full((1,), PAST_LEN), half)

    split = 2 * qk + 2 * vw
    wa = jnp.concatenate([_bf(w_in[:, :, :split]), _bf(w_in[:, :, split + rw_w:])], axis=-1)
    wb = _bf(w_in[:, :, split:split + rw_w])
    gate_block = split // (2 * d)
    assert gate_block * 2 * d == split

    tm_m = _pick(main_rows, (1024, 512, 256, 128, 64))
    tm_mix = _pick(main_rows, (512, 256, 128, 64))
    ret_chunk = _pick(sp, (4 * CHUNK, 2 * CHUNK, CHUNK))
    tm_proj = _pick(main_rows, (1024, 512, 256, 128, 64))
    tn_a = _pick(wa.shape[-1], (2048, 1024, 512, 256))
    tf = ffn_w1.shape[-1]
    seq_tm = _pick(sp, PREP_TILES)

    w_ret_b, w_rw_b, w_out_b = _bf(w_ret_out), _bf(w_rwkv_out), _bf(w_out)
    w1_b, w3_b, w2_b = _bf(ffn_w1), _bf(ffn_w3), _bf(ffn_w2)
    ret_p, wkv_p, sh_p, sh_s = [], [], [], []
    ret_s = wkv_s = None
    wkv_flat = state_wkv.reshape(depth, ns, wh * hd * hd)
    vf_m = vf_s = None
    for l in range(depth):
        g = norm_gain[l]
        p = dict(mu=rwkv_mu[l][None], w0=rwkv_w0[l][None], w2=_bf(rwkv_w2[l]), a0=rwkv_a0[l][None],
                 a2=_bf(rwkv_a2[l]), g2=_bf(rwkv_g2[l]), kk=rwkv_kk[l][None], ka=rwkv_ka[l][None])
        lv = max(l - 1, 0)
        p.update(v0=rwkv_v0[lv][None], v1=_bf(rwkv_v1[lv]), v2=_bf(rwkv_v2[lv]))

        pa_s = _norm_proj(xs, g[0][None], wa, l, small_rows, tn_a)
        prev_s = jnp.concatenate([jnp.zeros((CHUNK, rw_w), F32), state_shift[l]], axis=0)
        if l == 0:
            vf_s = jnp.zeros((small_rows, rc), BF16)
        r_s, lw_s, k_s, v_s, kk_s, b_s, g_s, pb_sf = _proj_prep(
            xs, g[0][None], wb, l, prev_s, p, vf_s, layer0=(l == 0), small=True, tm=small_rows,
            seq_rows=small_rows, hd=hd, n_pad=n_pad, n_head=CHUNK)
        if l == 0:
            vf_s = v_s

        og_meta, s_ret_meta = _ret_chunks(pa_s, cos_t, sin_t, jnp.zeros((rh, dk, dv), F32), chunk=CHUNK,
                                          n_seq=1, n_chunks=1, heads=rh, dk=dk, dv=dv, n_pad=n_pad)
        meta_in = [a[:CHUNK].reshape(1, CHUNK, rc) for a in (r_s, lw_s, k_s, v_s, kk_s, b_s)]
        y_meta, s_wkv_meta = _wkv_chunks(*meta_in, jnp.zeros((wh, hd, hd), F32), n_par=1, heads=wh, hd=hd,
                                         n_pad=n_pad)
        y_meta = y_meta.reshape(CHUNK, rc)
        pa_sf = pa_s[CHUNK:].astype(F32)
        og_smp, ret_s = _ret_step(pa_sf[:, :qk], pa_sf[:, qk:2 * qk], pa_sf[:, 2 * qk:2 * qk + vw],
                                  pa_sf[:, 2 * qk + vw:split], cos_s, sin_s, state_ret, ret_s, l,
                                  heads=rh, dk=dk, dv=dv, nb=_pick(ns, (2, 1)))
        smp = [a[CHUNK:].astype(F32).T for a in (r_s, lw_s, k_s, v_s, kk_s, b_s)]
        y_smp, wkv_s = _wkv_step(*smp, wkv_flat, wkv_s, l, heads=wh, hd=hd)
        og_s = jnp.concatenate([og_meta, _bf(og_smp)], axis=0)
        y_s = jnp.concatenate([y_meta, y_smp.T], axis=0)
        sh_s.append(pb_sf[CHUNK:])

        pa_m = _norm_proj(xm, g[0][None], wa, l, tm_proj, tn_a)
        if l == 0:
            vf_m = jnp.zeros((main_rows, rc), BF16)
        r_m, lw_m, k_m, v_m, kk_m, b_m, g_m, tails = _proj_prep(
            xm, g[0][None], wb, l, pb_sf[CHUNK - 1:CHUNK], p, vf_m, layer0=(l == 0), small=False, tm=seq_tm,
            seq_rows=sp, hd=hd)
        if l == 0:
            vf_m = v_m
        og_m, s_ret_m = _ret_chunks(pa_m, cos_m, sin_m, s_ret_meta[0], chunk=ret_chunk, n_seq=bp,
                                    n_chunks=sp // ret_chunk, heads=rh, dk=dk, dv=dv, n_pad=0)
        main_in = [a.reshape(bp, sp, rc) for a in (r_m, lw_m, k_m, v_m, kk_m, b_m)]
        y_m, s_wkv_m = _wkv_chunks(*main_in, s_wkv_meta[0], n_par=_pick(bp, (WKV_PAR, 1)), heads=wh, hd=hd,
                                   n_pad=0)
        y_m = y_m.reshape(main_rows, rc)
        ret_p.append(s_ret_m)
        wkv_p.append(s_wkv_m)
        sh_p.append(tails.reshape(bp, sp // seq_tm, rw_w)[:, -1])

        mixw = (w_ret_b, w_rw_b, w_out_b, rwkv_lnx_w[l][None], rwkv_lnx_b[l][None], rwkv_rk[l][None], g[1][None])
        ffnw = (g[2][None], g[3][None], w1_b, w3_b, w2_b)
        mix_kw = dict(layer=l, hd=hd, gate_block=gate_block)
        xs = _mix(og_s, y_s, r_s, k_s, v_s, g_s, pa_s, xs, *mixw, tm=small_rows, **mix_kw)
        xs = _ffn(xs, *ffnw, layer=l, tm=small_rows, tf=tf)
        xm = _mix(og_m, y_m, r_m, k_m, v_m, g_m, pa_m, xm, *mixw, tm=tm_mix, **mix_kw)
        xm = _ffn(xm, *ffnw, layer=l, tm=tm_m, tf=tf)

    return (xm.reshape(bp, sp, d), xs[CHUNK:].reshape(ns, 1, d), jnp.stack(ret_p), jnp.stack(wkv_p),
            jnp.stack(sh_p), ret_s, wkv_s.reshape(state_wkv.shape), jnp.stack(sh_s))
```
